```python
import math
import jax, jax.numpy as jnp
from jax import lax
import numpy as np

D_MODEL = 1024
BATCH = 4
SEQ = 8192
DEPTH = 2
DEC_BATCH = 2
DEC_SEQ = 8192
PAST_LEN = 128

N_GROUPS = 4
GROUP_W = D_MODEL // N_GROUPS
D_MIX = N_GROUPS * GROUP_W
HEAD_DIM = 64
N_HEADS_G = GROUP_W // HEAD_DIM
FNET_BLOCKS = 4
FNET_BLOCK_W = GROUP_W // FNET_BLOCKS
GDN_CHUNK = 64
CONV_K = 5
GRID_W = 64
NA_KH_MAX = 8
NA_KW = 16
N_MEM = 256
MEM_HEADS = 4
MEM_HEAD_DIM = GROUP_W // MEM_HEADS
EPS = 1e-6

N_GATE_COLS = 4 * N_HEADS_G
IN_SPLITS = (GROUP_W, 2 * GROUP_W, 5 * GROUP_W, 6 * GROUP_W, 6 * GROUP_W + N_GATE_COLS,
             9 * GROUP_W + N_GATE_COLS, 10 * GROUP_W + N_GATE_COLS, 11 * GROUP_W + N_GATE_COLS)
D_IN = 12 * GROUP_W + N_GATE_COLS

kernel_name = 'hybrid_fnet_gdn_natten_mem_encoder'


def rms_norm(x, g):
    xf = x.astype(jnp.float32)
    y = xf * lax.rsqrt(jnp.mean(xf * xf, axis=-1, keepdims=True) + EPS)
    return (y * g.astype(jnp.float32)).astype(x.dtype)


def l2_norm(x):
    return x * lax.rsqrt(jnp.sum(x * x, axis=-1, keepdims=True) + EPS)


def fourier_mix(u, w_fnet):
    b, l, _ = u.shape
    ub = u.astype(jnp.float32).reshape(b, l, FNET_BLOCKS, FNET_BLOCK_W)
    f = jnp.fft.fft2(ub, axes=(1, 3), norm='ortho').real
    return f.reshape(b, l, GROUP_W).astype(u.dtype) @ w_fnet


def centred_dwconv(x, w):
    pad = CONV_K // 2
    return lax.conv_general_dilated(x, w[:, None, :].astype(x.dtype), window_strides=(1,),
                                    padding=[(pad, pad)], dimension_numbers=('NWC', 'WIO', 'NWC'),
                                    feature_group_count=x.shape[-1])


def gdn_chunked(q, k, v, g, beta):
    b, l, h, dk = q.shape
    dv = v.shape[-1]
    c = GDN_CHUNK
    n = l // c

    def to_chunks(t):
        return jnp.moveaxis(t.reshape((b, n, c, h) + t.shape[3:]), 3, 1)

    q, k, v, g, beta = (to_chunks(t) for t in (q, k, v, g, beta))
    g = jnp.cumsum(g, axis=-1)
    idx = jnp.arange(c)
    incl = idx[:, None] >= idx[None, :]
    strict = idx[:, None] > idx[None, :]
    decay = jnp.exp(jnp.where(incl, g[..., :, None] - g[..., None, :], -jnp.inf))
    k_beta = k * beta[..., None]
    l_mat = jnp.where(strict, jnp.einsum('bhncd,bhnsd->bhncs', k_beta, k) * decay, 0.0)
    eye = jnp.eye(c, dtype=jnp.float32)
    rhs = jnp.concatenate([v * beta[..., None], k_beta * jnp.exp(g)[..., None]], axis=-1)
    sol = lax.linalg.triangular_solve(eye + l_mat, rhs, left_side=True, lower=True, unit_diagonal=True)
    u_c, w_c = sol[..., :dv], sol[..., dv:]
    qk = jnp.einsum('bhncd,bhnsd->bhncs', q, k) * decay
    q_dec = q * jnp.exp(g)[..., None]
    k_dec = k * jnp.exp(g[..., -1:] - g)[..., None]
    g_last = jnp.exp(g[..., -1])

    def step(state, xs):
        u_i, w_i, qk_i, qd_i, kd_i, gl_i = xs
        v_new = u_i - jnp.einsum('bhcd,bhde->bhce', w_i, state)
        o = jnp.einsum('bhcd,bhde->bhce', qd_i, state) + jnp.einsum('bhcs,bhse->bhce', qk_i, v_new)
        state = state * gl_i[..., None, None] + jnp.einsum('bhcd,bhce->bhde', kd_i, v_new)
        return state, o

    xs = tuple(jnp.moveaxis(t, 2, 0) for t in (u_c, w_c, qk, q_dec, k_dec, g_last))
    s0 = jnp.zeros((b, h, dk, dv), jnp.float32)
    _, o = lax.scan(step, s0, xs)
    o = jnp.moveaxis(o, 0, 2)
    return jnp.moveaxis(o, 1, 3).reshape(b, l, h, dv)


def gdn_branch(qkv, gates, conv_w, a_log, dt_bias, norm_g):
    b, l, _ = qkv.shape
    h = N_HEADS_G
    act = jax.nn.silu(centred_dwconv(qkv, conv_w)).astype(jnp.float32)
    q, k, v = jnp.split(act, 3, axis=-1)
    q = l2_norm(q.reshape(b, l, h, HEAD_DIM)) * (HEAD_DIM ** -0.5)
    k = l2_norm(k.reshape(b, l, h, HEAD_DIM))
    v = v.reshape(b, l, h, HEAD_DIM)
    gates = gates.astype(jnp.float32)
    a = gates[..., :2 * h].reshape(b, l, 2, h)
    beta = jax.nn.sigmoid(gates[..., 2 * h:].reshape(b, l, 2, h))
    g = -jnp.exp(a_log.astype(jnp.float32)) * jax.nn.softplus(a + dt_bias.astype(jnp.float32))
    o_f = gdn_chunked(q, k, v, g[:, :, 0], beta[:, :, 0])
    rev = lambda t: jnp.flip(t, axis=1)
    o_b = rev(gdn_chunked(rev(q), rev(k), rev(v), rev(g[:, :, 1]), rev(beta[:, :, 1])))
    o = rms_norm(o_f + o_b, norm_g)
    return o.reshape(b, l, GROUP_W).astype(qkv.dtype)


def neighbourhood_attn(q, k, v, rpb):
    b, l, h, d = q.shape
    rows = l // GRID_W
    kh = min(NA_KH_MAX, rows)
    qg = (q * (d ** -0.5)).reshape(b, rows, GRID_W, h, d)
    kg = k.reshape(b, rows, GRID_W, h, d)
    vg = v.reshape(b, rows, GRID_W, h, d)
    cols = np.arange(GRID_W)
    col_start = np.clip(cols - NA_KW // 2, 0, GRID_W - NA_KW)
    col_idx = col_start[:, None] + np.arange(NA_KW)[None, :]
    col_off = col_idx - cols[:, None] + (NA_KW - 1)
    rpb_cols = rpb[:, :, col_off]

    def row_block(r):
        rs = jnp.clip(r - kh // 2, 0, rows - kh)
        k_win = lax.dynamic_slice_in_dim(kg, rs, kh, axis=1)[:, :, col_idx]
        v_win = lax.dynamic_slice_in_dim(vg, rs, kh, axis=1)[:, :, col_idx]
        q_r = lax.dynamic_index_in_dim(qg, r, axis=1, keepdims=False)
        s = jnp.einsum('bwhd,biwjhd->bhwij', q_r, k_win).astype(jnp.float32)
        row_off = rs + jnp.arange(kh) - r + (NA_KH_MAX - 1)
        bias = jnp.take(rpb_cols, row_off, axis=1)
        s = s + jnp.transpose(bias, (0, 2, 1, 3))[None].astype(jnp.float32)
        p = jax.nn.softmax(s.reshape(b, h, GRID_W, kh * NA_KW), axis=-1)
        p = p.reshape(b, h, GRID_W, kh, NA_KW).astype(v.dtype)
        return jnp.einsum('bhwij,biwjhd->bwhd', p, v_win)

    out = lax.map(row_block, jnp.arange(rows))
    return jnp.moveaxis(out, 0, 1).reshape(b, l, h * d)


def memory_attn(q, mem_n, w_mem_kv):
    b, l, _ = q.shape
    m = mem_n.shape[1]
    km, vm = jnp.split(mem_n @ w_mem_kv, 2, axis=-1)
    km = km.reshape(b, m, MEM_HEADS, MEM_HEAD_DIM)
    vm = vm.reshape(b, m, MEM_HEADS, MEM_HEAD_DIM)
    qh = q.reshape(b, l, MEM_HEADS, MEM_HEAD_DIM)
    s = jnp.einsum('blhd,bmhd->bhlm', qh, km).astype(jnp.float32) * (MEM_HEAD_DIM ** -0.5)
    p = jax.nn.softmax(s, axis=-1).astype(q.dtype)
    return jnp.einsum('bhlm,bmhd->blhd', p, vm).reshape(b, l, GROUP_W)


def encoder_layer(x, mem, pre_g, post_g, w_in, w_fnet, conv_w, a_log, dt_bias, gdn_g, rpb, mem_g, w_mem_kv, w_out):
    b, l, _ = x.shape
    hx = rms_norm(x, pre_g)
    proj = hx @ w_in
    a_u, a_z, b_qkv, b_z, b_gates, c_qkv, c_z, d_q, d_z = jnp.split(proj, IN_SPLITS, axis=-1)
    y_a = fourier_mix(a_u, w_fnet) * jax.nn.silu(a_z)
    y_b = gdn_branch(b_qkv, b_gates, conv_w, a_log, dt_bias, gdn_g) * jax.nn.silu(b_z)
    cq, ck, cv = (t.reshape(b, l, N_HEADS_G, HEAD_DIM) for t in jnp.split(c_qkv, 3, axis=-1))
    y_c = neighbourhood_attn(cq, ck, cv, rpb) * jax.nn.silu(c_z)
    y_d = memory_attn(d_q, rms_norm(mem, mem_g), w_mem_kv) * jax.nn.silu(d_z)
    out = jnp.concatenate([y_a, y_b, y_c, y_d], axis=-1) @ w_out
    return x + rms_norm(out, post_g)


def setup_inputs(seed: int = 0) -> dict:
    key = jax.random.key(seed)
    ks = jax.random.split(key, 16)
    f32 = jnp.float32
    h = N_HEADS_G

    def nrm(k, shape, scale):
        return jax.random.normal(k, shape, f32) * scale

    dt = jnp.exp(jax.random.uniform(ks[12], (DEPTH, 2, h), f32, math.log(1e-3), math.log(1e-1)))
    return {
        'x_prompt': nrm(ks[0], (BATCH, SEQ, D_MODEL), 1.0),
        'x_sample': nrm(ks[1], (DEC_BATCH, DEC_SEQ, D_MODEL), 1.0),
        'mem_prompt': nrm(ks[2], (BATCH, N_MEM, D_MODEL), 1.0),
        'mem_sample': nrm(ks[3], (DEC_BATCH, N_MEM, D_MODEL), 1.0),
        'pre_norm_g': 1.0 + nrm(ks[4], (DEPTH, D_MODEL), 0.05),
        'post_norm_g': 1.0 + nrm(ks[5], (DEPTH, D_MODEL), 0.05),
        'w_in': nrm(ks[6], (DEPTH, D_MODEL, D_IN), D_MODEL ** -0.5),
        'w_fnet': nrm(ks[7], (DEPTH, GROUP_W, GROUP_W), GROUP_W ** -0.5),
        'gdn_conv_w': nrm(ks[8], (DEPTH, CONV_K, 3 * GROUP_W), CONV_K ** -0.5),
        'gdn_a_log': jnp.log(jax.random.uniform(ks[9], (DEPTH, 2, h), f32, 1.0, 16.0)),
        'gdn_dt_bias': dt + jnp.log(-jnp.expm1(-dt)),
        'gdn_norm_g': 1.0 + nrm(ks[10], (DEPTH, HEAD_DIM), 0.05),
        'na_rpb': nrm(ks[11], (DEPTH, h, 2 * NA_KH_MAX - 1, 2 * NA_KW - 1), 0.1),
        'mem_norm_g': 1.0 + nrm(ks[13], (DEPTH, D_MODEL), 0.05),
        'w_mem_kv': nrm(ks[14], (DEPTH, D_MODEL, 2 * GROUP_W), D_MODEL ** -0.5),
        'w_out': nrm(ks[15], (DEPTH, D_MIX, D_MODEL), D_MIX ** -0.5),
    }


def reference(x_prompt, x_sample, mem_prompt, mem_sample, pre_norm_g, post_norm_g, w_in, w_fnet,
              gdn_conv_w, gdn_a_log, gdn_dt_bias, gdn_norm_g, na_rpb, mem_norm_g, w_mem_kv, w_out):
    def trunk(x, mem):
        for i in range(DEPTH):
            x = encoder_layer(x, mem, pre_norm_g[i], post_norm_g[i], w_in[i], w_fnet[i], gdn_conv_w[i],
                              gdn_a_log[i], gdn_dt_bias[i], gdn_norm_g[i], na_rpb[i], mem_norm_g[i],
                              w_mem_kv[i], w_out[i])
        return x

    y_prompt = trunk(x_prompt, mem_prompt)
    y_sample = trunk(x_sample, mem_sample)
    return (y_prompt, y_sample)
```

```python
import functools
import math

import numpy as np
import jax
import jax.numpy as jnp
from jax import lax
from jax.experimental import pallas as pl
from jax.experimental.pallas import tpu as pltpu

F32 = jnp.float32
BF16 = jnp.bfloat16

D_MODEL = 1024
GROUP_W = 256
HEAD_DIM = 64
HEAD_SHIFT = 6
N_HEADS = 4
FNET_BLOCK_W = 64
GDN_CHUNK = 64
CONV_K = 5
GRID_W = 64
NA_KH = 8
NA_KW = 16
N_GATE_COLS = 16
EPS = 1e-6
NEG_BIG = -1e30

LANES = 128
SUBLANES = 8
GATE_PAD = LANES
D_PROJ = GROUP_W + 3 * GROUP_W + GATE_PAD + 3 * GROUP_W + GROUP_W + 4 * GROUP_W
FFT_NA = 64
VMEM_LIMIT = 56 * 1024 * 1024


def _cparams(sem):
    return pltpu.CompilerParams(dimension_semantics=sem, vmem_limit_bytes=VMEM_LIMIT)


def _dot(a, b):
    return jnp.dot(a.astype(BF16), b.astype(BF16), preferred_element_type=F32)


def _dot_nt(a, b):
    return lax.dot_general(a.astype(BF16), b.astype(BF16), (((1,), (1,)), ((), ())),
                           preferred_element_type=F32)


def _dot_split(a, b_bf16):
    hi = a.astype(BF16)
    lo = (a - hi.astype(F32)).astype(BF16)
    return (jnp.dot(hi, b_bf16, preferred_element_type=F32)
            + jnp.dot(lo, b_bf16, preferred_element_type=F32))


def _sigmoid(x):
    return 1.0 / (1.0 + jnp.exp(-x))


def _silu(x):
    return x * _sigmoid(x)


def _head_sum_matrix():
    idx = np.arange(GROUP_W) // HEAD_DIM
    return jnp.asarray((idx[:, None] == idx[None, :]).astype(np.float32), dtype=BF16)


def _inproj_kernel(x_ref, g_ref, w_ref, u_ref, bqkv_ref, bg_ref, cqkv_ref, dq_ref, z_ref):
    x = x_ref[...]
    ms = jnp.mean(x * x, axis=-1, keepdims=True)
    h = (x * lax.rsqrt(ms + EPS) * g_ref[...]).astype(BF16)
    off = 0
    for ref in (u_ref, bqkv_ref, bg_ref, cqkv_ref, dq_ref, z_ref):
        n = ref.shape[-1]
        ref[...] = jnp.dot(h, w_ref[:, off:off + n], preferred_element_type=F32)
        off += n


def _permute_w_in(w_in):
    g = GROUP_W
    o_az, o_bqkv, o_bz, o_bg = g, 2 * g, 5 * g, 6 * g
    o_cqkv = o_bg + N_GATE_COLS
    o_cz, o_dq, o_dz = o_cqkv + 3 * g, o_cqkv + 4 * g, o_cqkv + 5 * g
    pad = jnp.zeros((D_MODEL, GATE_PAD - N_GATE_COLS), w_in.dtype)
    cols = [w_in[:, 0:g], w_in[:, o_bqkv:o_bz], w_in[:, o_bg:o_cqkv], pad, w_in[:, o_cqkv:o_cz],
            w_in[:, o_dq:o_dz], w_in[:, o_az:o_bqkv], w_in[:, o_bz:o_bg], w_in[:, o_cz:o_dq],
            w_in[:, o_dz:o_dz + g]]
    return jnp.concatenate(cols, axis=1).astype(BF16)


def _inproj(x, pre_g, w_perm, tm):
    b, l, d = x.shape
    widths = (GROUP_W, 3 * GROUP_W, GATE_PAD, 3 * GROUP_W, GROUP_W, 4 * GROUP_W)
    row = lambda n: pl.BlockSpec((None, tm, n), lambda bi, i: (bi, i, 0))
    return pl.pallas_call(
        _inproj_kernel,
        grid=(b, l // tm),
        in_specs=[row(d),
                  pl.BlockSpec((1, d), lambda bi, i: (0, 0)),
                  pl.BlockSpec((d, D_PROJ), lambda bi, i: (0, 0))],
        out_specs=[row(n) for n in widths],
        out_shape=[jax.ShapeDtypeStruct((b, l, n), F32) for n in widths],
        compiler_params=_cparams(("parallel", "parallel")),
        name="inproj",
    )(x, pre_g.reshape(1, d), w_perm)


def _fnet_a_kernel(x_ref, t_ref, y_ref):
    y_ref[...] = jnp.dot(t_ref[...], x_ref[...].astype(BF16), preferred_element_type=F32).astype(BF16)


def _fnet_b_kernel(y_ref, g_ref, bcs_ref, wf_ref, o_ref):
    nb = o_ref.shape[0]
    z = jnp.dot(g_ref[...], y_ref[...], preferred_element_type=F32)
    f = (jnp.dot(z[:nb].astype(BF16), bcs_ref[:GROUP_W, :], preferred_element_type=F32)
         + jnp.dot(z[nb:].astype(BF16), bcs_ref[GROUP_W:, :], preferred_element_type=F32))
    o_ref[...] = jnp.dot(f.astype(BF16), wf_ref[...], preferred_element_type=F32)


def _fnet_tables(l):
    na, nb = FFT_NA, l // FFT_NA
    ia = np.arange(na)
    ang_a = 2.0 * np.pi * ((ia[:, None] * ia[None, :]) % na) / na
    ta = np.concatenate([np.cos(ang_a), -np.sin(ang_a)], axis=0) / math.sqrt(na)
    ka = jnp.arange(na, dtype=jnp.int32)[:, None, None]
    kb = jnp.arange(nb, dtype=jnp.int32)[None, :, None]
    n2 = jnp.arange(nb, dtype=jnp.int32)[None, None, :]
    ang = (2.0 * math.pi / l) * ((n2 * (ka + na * kb)) % l).astype(F32)
    cg = jnp.cos(ang) / math.sqrt(nb)
    sg = jnp.sin(ang) / math.sqrt(nb)
    gbig = jnp.concatenate([jnp.concatenate([cg, sg], axis=2),
                            jnp.concatenate([-sg, cg], axis=2)], axis=1).astype(BF16)
    ic = np.arange(GROUP_W)
    same = (ic[:, None] // FNET_BLOCK_W) == (ic[None, :] // FNET_BLOCK_W)
    ang_c = 2.0 * np.pi * (((ic[:, None] % FNET_BLOCK_W) * (ic[None, :] % FNET_BLOCK_W)) % FNET_BLOCK_W) / FNET_BLOCK_W
    bc = np.where(same, np.cos(ang_c), 0.0) / math.sqrt(FNET_BLOCK_W)
    bs = np.where(same, np.sin(ang_c), 0.0) / math.sqrt(FNET_BLOCK_W)
    bcs = np.concatenate([bc, bs], axis=0)
    return jnp.asarray(ta, dtype=F32).astype(BF16), gbig, jnp.asarray(bcs, dtype=F32).astype(BF16)


def _fnet(u, w_fnet, tables):
    b, l, c = u.shape
    na, nb = FFT_NA, l // FFT_NA
    ta, gbig, bcs = tables
    cw = min(2048, nb * c)
    x2 = u.reshape(b, na, nb * c)
    y = pl.pallas_call(
        _fnet_a_kernel,
        grid=(b, nb * c // cw),
        in_specs=[pl.BlockSpec((None, na, cw), lambda bi, j: (bi, 0, j)),
                  pl.BlockSpec((2 * na, na), lambda bi, j: (0, 0))],
        out_specs=pl.BlockSpec((None, 2 * na, cw), lambda bi, j: (bi, 0, j)),
        out_shape=jax.ShapeDtypeStruct((b, 2 * na, nb * c), BF16),
        compiler_params=_cparams(("parallel", "parallel")),
        name="fnet_a",
    )(x2, ta)
    ys = y.reshape(b, 2, na, nb, c).transpose(0, 2, 1, 3, 4).reshape(b, na, 2 * nb, c)
    o = pl.pallas_call(
        _fnet_b_kernel,
        grid=(b, na),
        in_specs=[pl.BlockSpec((None, None, 2 * nb, c), lambda bi, k: (bi, k, 0, 0)),
                  pl.BlockSpec((None, 2 * nb, 2 * nb), lambda bi, k: (k, 0, 0)),
                  pl.BlockSpec((2 * c, c), lambda bi, k: (0, 0)),
                  pl.BlockSpec((c, c), lambda bi, k: (0, 0))],
        out_specs=pl.BlockSpec((None, None, nb, c), lambda bi, k: (bi, k, 0, 0)),
        out_shape=jax.ShapeDtypeStruct((b, na, nb, c), F32),
        compiler_params=_cparams(("parallel", "parallel")),
        name="fnet_b",
    )(ys, gbig, bcs, w_fnet.astype(BF16))
    return o.transpose(0, 2, 1, 3).reshape(b, l, c)


def _gdn_pre_kernel(cur_ref, prev_ref, next_ref, gate_ref, cw_ref, gp_ref, hs_ref, qkv_ref, gb_ref, win_ref):
    i = pl.program_id(1)
    n = pl.num_programs(1)
    tg = cur_ref.shape[0]
    halo = prev_ref.shape[0]
    win_ref[0:halo, :] = prev_ref[...] * (i > 0).astype(F32)
    win_ref[halo:halo + tg, :] = cur_ref[...]
    win_ref[halo + tg:, :] = next_ref[...] * (i < n - 1).astype(F32)
    pad = CONV_K // 2
    acc = cw_ref[0:1, :] * win_ref[pl.ds(halo - pad, tg), :]
    for t in range(1, CONV_K):
        acc = acc + cw_ref[t:t + 1, :] * win_ref[pl.ds(halo - pad + t, tg), :]
    act = _silu(acc)
    hs = hs_ref[...]
    q = act[:, 0:GROUP_W]
    k = act[:, GROUP_W:2 * GROUP_W]
    qn = q * lax.rsqrt(_dot_split(q * q, hs) + EPS) * (HEAD_DIM ** -0.5)
    kn = k * lax.rsqrt(_dot_split(k * k, hs) + EPS)
    qkv_ref[:, 0:GROUP_W] = qn
    qkv_ref[:, GROUP_W:2 * GROUP_W] = kn
    qkv_ref[:, 2 * GROUP_W:] = act[:, 2 * GROUP_W:]
    a = gate_ref[...]
    x = a + gp_ref[1:2, :]
    softplus = jnp.maximum(x, 0.0) + jnp.log(1.0 + jnp.exp(-jnp.abs(x)))
    g = -jnp.exp(gp_ref[0:1, :]) * softplus
    lane = lax.broadcasted_iota(jnp.int32, a.shape, 1)
    gb_ref[...] = jnp.where(lane < N_GATE_COLS // 2, g, _sigmoid(a))


def _gdn_pre(bqkv, gates, conv_w, a_log, dt_bias, tg):
    b, l, c3 = bqkv.shape
    halo = SUBLANES
    nt = l // tg
    per = tg // halo
    nh = l // halo
    cw = jnp.zeros((SUBLANES, c3), F32).at[:CONV_K].set(conv_w)
    gp = jnp.zeros((SUBLANES, GATE_PAD), F32)
    gp = gp.at[0, :N_GATE_COLS // 2].set(a_log.reshape(-1)).at[1, :N_GATE_COLS // 2].set(dt_bias.reshape(-1))
    return pl.pallas_call(
        _gdn_pre_kernel,
        grid=(b, nt),
        in_specs=[pl.BlockSpec((None, tg, c3), lambda bi, i: (bi, i, 0)),
                  pl.BlockSpec((None, halo, c3), lambda bi, i: (bi, jnp.maximum(i * per - 1, 0), 0)),
                  pl.BlockSpec((None, halo, c3), lambda bi, i: (bi, jnp.minimum((i + 1) * per, nh - 1), 0)),
                  pl.BlockSpec((None, tg, GATE_PAD), lambda bi, i: (bi, i, 0)),
                  pl.BlockSpec((SUBLANES, c3), lambda bi, i: (0, 0)),
                  pl.BlockSpec((SUBLANES, GATE_PAD), lambda bi, i: (0, 0)),
                  pl.BlockSpec((GROUP_W, GROUP_W), lambda bi, i: (0, 0))],
        out_specs=[pl.BlockSpec((None, tg, c3), lambda bi, i: (bi, i, 0)),
                   pl.BlockSpec((None, tg, GATE_PAD), lambda bi, i: (bi, i, 0))],
        out_shape=[jax.ShapeDtypeStruct((b, l, c3), F32),
                   jax.ShapeDtypeStruct((b, l, GATE_PAD), F32)],
        scratch_shapes=[pltpu.VMEM((tg + 2 * halo, c3), F32)],
        compiler_params=_cparams(("parallel", "parallel")),
        name="gdn_pre",
    )(bqkv, bqkv, bqkv, gates, cw, gp, _head_sum_matrix())


def _unit_tri_inverse(lm, row, col):
    c = lm.shape[0]
    eye = (row == col).astype(F32)
    minv = eye - jnp.where((row >> 1) == (col >> 1), lm, 0.0)
    sh = 1
    while (2 << sh) <= c:
        blk = jnp.where(((row >> (sh + 1)) == (col >> (sh + 1))) & ((row >> sh) != (col >> sh)), lm, 0.0)
        minv = minv - _dot(_dot(minv, blk), minv)
        sh += 1
    return minv


def _gdn_dir(x, gb, s_ref, d):
    c = GDN_CHUNK
    row = lax.broadcasted_iota(jnp.int32, (c, c), 0)
    col = lax.broadcasted_iota(jnp.int32, (c, c), 1)
    incl = (row >= col) if d == 0 else (row <= col)
    strict = (row > col) if d == 0 else (row < col)
    eye_b = (row == col).astype(BF16)
    gc = _dot_split_left(incl.astype(BF16), gb)
    gct = gc.T
    last = c - 1 if d == 0 else 0
    g_last = gc[last:last + 1, :]
    e_g = jnp.exp(gc)
    e_rem = jnp.exp(g_last - gc)
    e_last = jnp.exp(g_last)
    outs = []
    for h in range(N_HEADS):
        j = N_HEADS * d + h
        q = x[:, h * HEAD_DIM:(h + 1) * HEAD_DIM]
        k = x[:, GROUP_W + h * HEAD_DIM:GROUP_W + (h + 1) * HEAD_DIM]
        v = x[:, 2 * GROUP_W + h * HEAD_DIM:2 * GROUP_W + (h + 1) * HEAD_DIM]
        beta = gb[:, N_GATE_COLS // 2 + j:N_GATE_COLS // 2 + j + 1]
        decay = jnp.exp(jnp.where(incl, gc[:, j:j + 1] - gct[j:j + 1, :], -jnp.inf))
        kb = k * beta
        lm = jnp.where(strict, _dot_nt(kb, k) * decay, 0.0)
        minv = _unit_tri_inverse(lm, row, col)
        rhs = jnp.concatenate([v * beta, kb * e_g[:, j:j + 1]], axis=1)
        sol = _dot(minv, rhs)
        u = sol[:, :HEAD_DIM]
        w = sol[:, HEAD_DIM:]
        qk = _dot_nt(q, k) * decay
        qd = q * e_g[:, j:j + 1]
        kd = k * e_rem[:, j:j + 1]
        state = s_ref[j]
        v_new = u - _dot(w, state)
        outs.append(_dot(qd, state) + _dot(qk, v_new))
        kdt = _dot_nt(eye_b, kd)
        s_ref[j] = state * e_last[:, j:j + 1] + _dot(kdt, v_new)
    return jnp.concatenate(outs, axis=1)


def _dot_split_left(m_bf16, a):
    hi = a.astype(BF16)
    lo = (a - hi.astype(F32)).astype(BF16)
    return (jnp.dot(m_bf16, hi, preferred_element_type=F32)
            + jnp.dot(m_bf16, lo, preferred_element_type=F32))


def _gdn_scan_kernel(xf_ref, gbf_ref, xb_ref, gbb_ref, of_ref, ob_ref, s_ref):
    @pl.when(pl.program_id(1) == 0)
    def _():
        s_ref[...] = jnp.zeros_like(s_ref)

    of_ref[...] = _gdn_dir(xf_ref[...], gbf_ref[...], s_ref, 0)
    ob_ref[...] = _gdn_dir(xb_ref[...], gbb_ref[...], s_ref, 1)


def _gdn_scan(qkvn, gb):
    b, l, c3 = qkvn.shape
    c = GDN_CHUNK
    nc = l // c
    fwd = lambda w: pl.BlockSpec((None, c, w), lambda bi, i: (bi, i, 0))
    bwd = lambda w: pl.BlockSpec((None, c, w), lambda bi, i: (bi, nc - 1 - i, 0))
    return pl.pallas_call(
        _gdn_scan_kernel,
        grid=(b, nc),
        in_specs=[fwd(c3), fwd(GATE_PAD), bwd(c3), bwd(GATE_PAD)],
        out_specs=[fwd(GROUP_W), bwd(GROUP_W)],
        out_shape=[jax.ShapeDtypeStruct((b, l, GROUP_W), F32)] * 2,
        scratch_shapes=[pltpu.VMEM((2 * N_HEADS, HEAD_DIM, HEAD_DIM), F32)],
        compiler_params=_cparams(("parallel", "arbitrary")),
        name="gdn_scan",
    )(qkvn, gb, qkvn, gb)


def _head_stack(q):
    lane_head = lax.broadcasted_iota(jnp.int32, q.shape, 1) >> HEAD_SHIFT
    return jnp.concatenate([jnp.where(lane_head == h, q, 0.0) for h in range(N_HEADS)], axis=0)


def _head_unstack(pv, m):
    lane_head = lax.broadcasted_iota(jnp.int32, (m, GROUP_W), 1) >> HEAD_SHIFT
    out = jnp.where(lane_head == 0, pv[0:m], 0.0)
    for h in range(1, N_HEADS):
        out = out + jnp.where(lane_head == h, pv[h * m:(h + 1) * m], 0.0)
    return out


def _softmax_pv(s, v_bf16):
    m = jnp.max(s, axis=-1, keepdims=True)
    p = jnp.exp(s - m)
    den = jnp.sum(p, axis=-1, keepdims=True)
    return jnp.dot(p.astype(BF16), v_bf16, preferred_element_type=F32) / den


def _natten_kernel(q_ref, kp_ref, kc_ref, kn_ref, vp_ref, vc_ref, vn_ref, bias_ref, o_ref, kwin_ref, vwin_ref):
    gi = pl.program_id(1)
    rows = pl.num_programs(1) * NA_KH
    blk = kc_ref.shape[0]
    for t, (kr, vr) in enumerate(((kp_ref, vp_ref), (kc_ref, vc_ref), (kn_ref, vn_ref))):
        kwin_ref[t * blk:(t + 1) * blk, :] = kr[...].astype(BF16)
        vwin_ref[t * blk:(t + 1) * blk, :] = vr[...].astype(BF16)

    def body(j, carry):
        r = gi * NA_KH + j
        rs = jnp.clip(r - NA_KH // 2, 0, rows - NA_KH)
        start = pl.multiple_of((rs - (gi - 1) * NA_KH) * GRID_W, GRID_W)
        q = q_ref[pl.ds(pl.multiple_of(j * GRID_W, GRID_W), GRID_W), :] * (HEAD_DIM ** -0.5)
        kw = kwin_ref[pl.ds(start, NA_KH * GRID_W), :]
        vw = vwin_ref[pl.ds(start, NA_KH * GRID_W), :]
        s = _dot_nt(_head_stack(q), kw) + bias_ref[r - rs]
        o_ref[pl.ds(pl.multiple_of(j * GRID_W, GRID_W), GRID_W), :] = _head_unstack(_softmax_pv(s, vw), GRID_W)
        return carry

    lax.fori_loop(0, NA_KH, body, 0)


def _natten_bias(rpb):
    w = np.arange(GRID_W)
    cs = np.clip(w - NA_KW // 2, 0, GRID_W - NA_KW)
    wk = np.arange(GRID_W)
    in_win = (wk[None, :] >= cs[:, None]) & (wk[None, :] < cs[:, None] + NA_KW)
    col_off = np.clip(wk[None, :] - w[:, None] + NA_KW - 1, 0, 2 * NA_KW - 2)
    delta = np.arange(NA_KH)
    i = np.arange(NA_KH)
    row_off = i[None, :] - delta[:, None] + NA_KH - 1
    t = rpb[:, row_off]
    t = t[:, :, :, col_off]
    t = jnp.where(jnp.asarray(in_win)[None, None, None], t, NEG_BIG)
    t = t.transpose(1, 0, 3, 2, 4)
    return t.reshape(NA_KH, N_HEADS * GRID_W, NA_KH * GRID_W).astype(F32)


def _natten(cqkv, rpb):
    b, l, _ = cqkv.shape
    blk = NA_KH * GRID_W
    ng = l // blk
    bias = _natten_bias(rpb)
    spec = lambda colblk, f: pl.BlockSpec((None, blk, GROUP_W), lambda bi, g: (bi, f(g), colblk))
    cur = lambda g: g
    prev = lambda g: jnp.maximum(g - 1, 0)
    nxt = lambda g: jnp.minimum(g + 1, ng - 1)
    return pl.pallas_call(
        _natten_kernel,
        grid=(b, ng),
        in_specs=[spec(0, cur), spec(1, prev), spec(1, cur), spec(1, nxt),
                  spec(2, prev), spec(2, cur), spec(2, nxt),
                  pl.BlockSpec(bias.shape, lambda bi, g: (0, 0, 0))],
        out_specs=pl.BlockSpec((None, blk, GROUP_W), lambda bi, g: (bi, g, 0)),
        out_shape=jax.ShapeDtypeStruct((b, l, GROUP_W), F32),
        scratch_shapes=[pltpu.VMEM((3 * blk, GROUP_W), BF16), pltpu.VMEM((3 * blk, GROUP_W), BF16)],
        compiler_params=_cparams(("parallel", "parallel")),
        name="natten",
    )(cqkv, cqkv, cqkv, cqkv, cqkv, cqkv, cqkv, bias)


def _memattn_kernel(q_ref, mem_ref, mg_ref, wkv_ref, o_ref, k_ref, v_ref):
    @pl.when(pl.program_id(1) == 0)
    def _():
        m = mem_ref[...]
        ms = jnp.mean(m * m, axis=-1, keepdims=True)
        mn = (m * lax.rsqrt(ms + EPS) * mg_ref[...]).astype(BF16)
        kv = jnp.dot(mn, wkv_ref[...], preferred_element_type=F32)
        k_ref[...] = kv[:, :GROUP_W].astype(BF16)
        v_ref[...] = kv[:, GROUP_W:].astype(BF16)

    tq = q_ref.shape[0]
    q = q_ref[...] * (HEAD_DIM ** -0.5)
    s = _dot_nt(_head_stack(q), k_ref[...])
    o_ref[...] = _head_unstack(_softmax_pv(s, v_ref[...]), tq)


def _memattn(dq, mem, mem_g, w_mem_kv, tq):
    b, l, c = dq.shape
    m, d = mem.shape[1], mem.shape[2]
    return pl.pallas_call(
        _memattn_kernel,
        grid=(b, l // tq),
        in_specs=[pl.BlockSpec((None, tq, c), lambda bi, i: (bi, i, 0)),
                  pl.BlockSpec((None, m, d), lambda bi, i: (bi, 0, 0)),
                  pl.BlockSpec((1, d), lambda bi, i: (0, 0)),
                  pl.BlockSpec((d, 2 * c), lambda bi, i: (0, 0))],
        out_specs=pl.BlockSpec((None, tq, c), lambda bi, i: (bi, i, 0)),
        out_shape=jax.ShapeDtypeStruct((b, l, c), F32),
        scratch_shapes=[pltpu.VMEM((m, c), BF16), pltpu.VMEM((m, c), BF16)],
        compiler_params=_cparams(("parallel", "arbitrary")),
        name="memattn",
    )(dq, mem, mem_g.reshape(1, d), w_mem_kv.astype(BF16))


def _outproj_kernel(ya_ref, of_ref, ob_ref, yc_ref, yd_ref, z_ref, x_ref, gg_ref, pg_ref, hs_ref, w_ref, o_ref):
    o = of_ref[...] + ob_ref[...]
    ms = _dot_split(o * o, hs_ref[...]) * (1.0 / HEAD_DIM)
    yb = o * lax.rsqrt(ms + EPS) * gg_ref[...]
    acc = None
    for i, y in enumerate((ya_ref[...], yb, yc_ref[...], yd_ref[...])):
        gated = (y * _silu(z_ref[:, i * GROUP_W:(i + 1) * GROUP_W])).astype(BF16)
        part = jnp.dot(gated, w_ref[i * GROUP_W:(i + 1) * GROUP_W, :], preferred_element_type=F32)
        acc = part if acc is None else acc + part
    ms = jnp.mean(acc * acc, axis=-1, keepdims=True)
    o_ref[...] = x_ref[...] + acc * lax.rsqrt(ms + EPS) * pg_ref[...]


def _outproj(ya, o_f, o_b, yc, yd, z, x, gdn_g, post_g, w_out, tm):
    b, l, d = x.shape
    grp = pl.BlockSpec((None, tm, GROUP_W), lambda bi, i: (bi, i, 0))
    full = pl.BlockSpec((None, tm, d), lambda bi, i: (bi, i, 0))
    const = lambda shape: pl.BlockSpec(shape, lambda bi, i: (0, 0))
    return pl.pallas_call(
        _outproj_kernel,
        grid=(b, l // tm),
        in_specs=[grp, grp, grp, grp, grp, full, full,
                  const((1, GROUP_W)), const((1, d)), const((GROUP_W, GROUP_W)), const((d, d))],
        out_specs=full,
        out_shape=jax.ShapeDtypeStruct((b, l, d), F32),
        compiler_params=_cparams(("parallel", "parallel")),
        name="outproj",
    )(ya, o_f, o_b, yc, yd, z, x, jnp.tile(gdn_g, N_HEADS).reshape(1, GROUP_W), post_g.reshape(1, d),
      _head_sum_matrix(), w_out.astype(BF16))


def _tile(l, want):
    t = min(want, l)
    assert l % t == 0
    return t


def _layer(x, mem, tables, pre_g, post_g, w_in, w_fnet, conv_w, a_log, dt_bias, gdn_g, rpb, mem_g, w_mem_kv, w_out):
    l = x.shape[1]
    u, bqkv, bgate, cqkv, dq, z = _inproj(x, pre_g, _permute_w_in(w_in), _tile(l, 512))
    ya = _fnet(u, w_fnet, tables)
    qkvn, gb = _gdn_pre(bqkv, bgate, conv_w, a_log, dt_bias, _tile(l, 512))
    o_f, o_b = _gdn_scan(qkvn, gb)
    yc = _natten(cqkv, rpb)
    yd = _memattn(dq, mem, mem_g, w_mem_kv, _tile(l, 256))
    return _outproj(ya, o_f, o_b, yc, yd, z, x, gdn_g, post_g, w_out, _tile(l, 512))


def _trunk(x, mem, pre_norm_g, post_norm_g, w_in, w_fnet, gdn_conv_w, gdn_a_log, gdn_dt_bias, gdn_norm_g,
           na_rpb, mem_norm_g, w_mem_kv, w_out):
    l = x.shape[1]
    assert x.shape[2] == D_MODEL and l % (NA_KH * GRID_W) == 0 and l % FFT_NA == 0
    tables = _fnet_tables(l)
    for i in range(pre_norm_g.shape[0]):
        x = _layer(x, mem, tables, pre_norm_g[i], post_norm_g[i], w_in[i], w_fnet[i], gdn_conv_w[i],
                   gdn_a_log[i], gdn_dt_bias[i], gdn_norm_g[i], na_rpb[i], mem_norm_g[i], w_mem_kv[i], w_out[i])
    return x


def kernel(x_prompt, x_sample, mem_prompt, mem_sample, pre_norm_g, post_norm_g, w_in, w_fnet, gdn_conv_w,
           gdn_a_log, gdn_dt_bias, gdn_norm_g, na_rpb, mem_norm_g, w_mem_kv, w_out):
    assert x_prompt.shape[1:] == x_sample.shape[1:] and mem_prompt.shape[1:] == mem_sample.shape[1:]
    nb = x_prompt.shape[0]
    x = jnp.concatenate([x_prompt, x_sample], axis=0)
    mem = jnp.concatenate([mem_prompt, mem_sample], axis=0)
    y = _trunk(x, mem, pre_norm_g, post_norm_g, w_in, w_fnet, gdn_conv_w, gdn_a_log, gdn_dt_bias, gdn_norm_g,
               na_rpb, mem_norm_g, w_mem_kv, w_out)
    return (y[:nb], y[nb:])
```

```python
import functools
import math

import numpy as np
import jax
import jax.numpy as jnp
from jax import lax
from jax.experimental import pallas as pl
from jax.experimental.pallas import tpu as pltpu

F32 = jnp.float32
BF16 = jnp.bfloat16

D_MODEL = 1024
GROUP_W = 256
HEAD_DIM = 64
HEAD_SHIFT = 6
N_HEADS = 4
FNET_BLOCK_W = 64
GDN_CHUNK = 64
CONV_K = 5
GRID_W = 64
NA_KH = 8
NA_KW = 16
N_GATE_COLS = 16
EPS = 1e-6
NEG_BIG = -1e30

LANES = 128
SUBLANES = 8
GATE_PAD = LANES
D_PROJ = GROUP_W + 3 * GROUP_W + GATE_PAD + 3 * GROUP_W + GROUP_W + 4 * GROUP_W
FFT_NA = 64
GDN_CHUNKS_PER_STEP = 4
NA_ROWS_PER_ITER = 2
MEM_Q_SPLIT = 2
VMEM_LIMIT = 56 * 1024 * 1024


def _cparams(sem):
    return pltpu.CompilerParams(dimension_semantics=sem, vmem_limit_bytes=VMEM_LIMIT)


def _dot(a, b):
    return jnp.dot(a.astype(BF16), b.astype(BF16), preferred_element_type=F32)


def _dot_nt(a, b):
    return lax.dot_general(a.astype(BF16), b.astype(BF16), (((1,), (1,)), ((), ())),
                           preferred_element_type=F32)


def _dot_split(a, b_bf16):
    hi = a.astype(BF16)
    lo = (a - hi.astype(F32)).astype(BF16)
    return (jnp.dot(hi, b_bf16, preferred_element_type=F32)
            + jnp.dot(lo, b_bf16, preferred_element_type=F32))


def _sigmoid(x):
    return 1.0 / (1.0 + jnp.exp(-x))


def _silu(x):
    return x * _sigmoid(x)


def _head_sum_matrix():
    idx = np.arange(GROUP_W) // HEAD_DIM
    return jnp.asarray((idx[:, None] == idx[None, :]).astype(np.float32), dtype=BF16)


def _inproj_kernel(x_ref, g_ref, w_ref, u_ref, bqkv_ref, bg_ref, cqkv_ref, dq_ref, z_ref):
    x = x_ref[...]
    ms = jnp.mean(x * x, axis=-1, keepdims=True)
    h = (x * lax.rsqrt(ms + EPS) * g_ref[...]).astype(BF16)
    off = 0
    for ref in (u_ref, bqkv_ref, bg_ref, cqkv_ref, dq_ref, z_ref):
        n = ref.shape[-1]
        ref[...] = jnp.dot(h, w_ref[:, off:off + n], preferred_element_type=F32).astype(ref.dtype)
        off += n


def _permute_w_in(w_in):
    g = GROUP_W
    o_az, o_bqkv, o_bz, o_bg = g, 2 * g, 5 * g, 6 * g
    o_cqkv = o_bg + N_GATE_COLS
    o_cz, o_dq, o_dz = o_cqkv + 3 * g, o_cqkv + 4 * g, o_cqkv + 5 * g
    pad = jnp.zeros((D_MODEL, GATE_PAD - N_GATE_COLS), w_in.dtype)
    cols = [w_in[:, 0:g], w_in[:, o_bqkv:o_bz], w_in[:, o_bg:o_cqkv], pad, w_in[:, o_cqkv:o_cz],
            w_in[:, o_dq:o_dz], w_in[:, o_az:o_bqkv], w_in[:, o_bz:o_bg], w_in[:, o_cz:o_dq],
            w_in[:, o_dz:o_dz + g]]
    return jnp.concatenate(cols, axis=1).astype(BF16)


def _inproj(x, pre_g, w_perm, tm):
    b, l, d = x.shape
    widths = (GROUP_W, 3 * GROUP_W, GATE_PAD, 3 * GROUP_W, GROUP_W, 4 * GROUP_W)
    dtypes = (BF16, F32, F32, BF16, BF16, F32)
    row = lambda n: pl.BlockSpec((None, tm, n), lambda bi, i: (bi, i, 0))
    return pl.pallas_call(
        _inproj_kernel,
        grid=(b, l // tm),
        in_specs=[row(d),
                  pl.BlockSpec((1, d), lambda bi, i: (0, 0)),
                  pl.BlockSpec((d, D_PROJ), lambda bi, i: (0, 0))],
        out_specs=[row(n) for n in widths],
        out_shape=[jax.ShapeDtypeStruct((b, l, n), dt) for n, dt in zip(widths, dtypes)],
        compiler_params=_cparams(("parallel", "parallel")),
        name="inproj",
    )(x, pre_g.reshape(1, d), w_perm)


def _fnet_a_kernel(x_ref, t_ref, y_ref):
    y_ref[...] = jnp.dot(t_ref[...], x_ref[...].astype(BF16), preferred_element_type=F32).astype(BF16)


def _fnet_b_kernel(y_ref, g_ref, bcs_ref, wf_ref, o_ref):
    nb = o_ref.shape[0]
    z = jnp.dot(g_ref[...], y_ref[...], preferred_element_type=F32)
    f = (jnp.dot(z[:nb].astype(BF16), bcs_ref[:GROUP_W, :], preferred_element_type=F32)
         + jnp.dot(z[nb:].astype(BF16), bcs_ref[GROUP_W:, :], preferred_element_type=F32))
    o_ref[...] = jnp.dot(f.astype(BF16), wf_ref[...], preferred_element_type=F32)


def _fnet_tables(l):
    na, nb = FFT_NA, l // FFT_NA
    ia = np.arange(na)
    ang_a = 2.0 * np.pi * ((ia[:, None] * ia[None, :]) % na) / na
    ta = np.concatenate([np.cos(ang_a), -np.sin(ang_a)], axis=0) / math.sqrt(na)
    ka = jnp.arange(na, dtype=jnp.int32)[:, None, None]
    kb = jnp.arange(nb, dtype=jnp.int32)[None, :, None]
    n2 = jnp.arange(nb, dtype=jnp.int32)[None, None, :]
    ang = (2.0 * math.pi / l) * ((n2 * (ka + na * kb)) % l).astype(F32)
    cg = jnp.cos(ang) / math.sqrt(nb)
    sg = jnp.sin(ang) / math.sqrt(nb)
    gbig = jnp.concatenate([jnp.concatenate([cg, sg], axis=2),
                            jnp.concatenate([-sg, cg], axis=2)], axis=1).astype(BF16)
    ic = np.arange(GROUP_W)
    same = (ic[:, None] // FNET_BLOCK_W) == (ic[None, :] // FNET_BLOCK_W)
    ang_c = 2.0 * np.pi * (((ic[:, None] % FNET_BLOCK_W) * (ic[None, :] % FNET_BLOCK_W)) % FNET_BLOCK_W) / FNET_BLOCK_W
    bc = np.where(same, np.cos(ang_c), 0.0) / math.sqrt(FNET_BLOCK_W)
    bs = np.where(same, np.sin(ang_c), 0.0) / math.sqrt(FNET_BLOCK_W)
    bcs = np.concatenate([bc, bs], axis=0)
    return jnp.asarray(ta, dtype=F32).astype(BF16), gbig, jnp.asarray(bcs, dtype=F32).astype(BF16)


def _fnet(u, w_fnet, tables):
    b, l, c = u.shape
    na, nb = FFT_NA, l // FFT_NA
    ta, gbig, bcs = tables
    cw = min(2048, nb * c)
    x2 = u.reshape(b, na, nb * c)
    y = pl.pallas_call(
        _fnet_a_kernel,
        grid=(b, nb * c // cw),
        in_specs=[pl.BlockSpec((None, na, cw), lambda bi, j: (bi, 0, j)),
                  pl.BlockSpec((2 * na, na), lambda bi, j: (0, 0))],
        out_specs=pl.BlockSpec((None, 2 * na, cw), lambda bi, j: (bi, 0, j)),
        out_shape=jax.ShapeDtypeStruct((b, 2 * na, nb * c), BF16),
        compiler_params=_cparams(("parallel", "parallel")),
        name="fnet_a",
    )(x2, ta)
    ys = y.reshape(b, 2, na, nb, c).transpose(0, 2, 1, 3, 4).reshape(b, na, 2 * nb, c)
    o = pl.pallas_call(
        _fnet_b_kernel,
        grid=(b, na),
        in_specs=[pl.BlockSpec((None, None, 2 * nb, c), lambda bi, k: (bi, k, 0, 0)),
                  pl.BlockSpec((None, 2 * nb, 2 * nb), lambda bi, k: (k, 0, 0)),
                  pl.BlockSpec((2 * c, c), lambda bi, k: (0, 0)),
                  pl.BlockSpec((c, c), lambda bi, k: (0, 0))],
        out_specs=pl.BlockSpec((None, None, nb, c), lambda bi, k: (bi, k, 0, 0)),
        out_shape=jax.ShapeDtypeStruct((b, na, nb, c), F32),
        compiler_params=_cparams(("parallel", "parallel")),
        name="fnet_b",
    )(ys, gbig, bcs, w_fnet.astype(BF16))
    return o.transpose(0, 2, 1, 3).reshape(b, l, c)


def _gdn_pre_kernel(cur_ref, prev_ref, next_ref, gate_ref, cw_ref, gp_ref, hs_ref, tri_ref, qkv_ref, gb_ref, win_ref):
    i = pl.program_id(1)
    n = pl.num_programs(1)
    tg = cur_ref.shape[0]
    halo = prev_ref.shape[0]
    win_ref[0:halo, :] = prev_ref[...] * (i > 0).astype(F32)
    win_ref[halo:halo + tg, :] = cur_ref[...]
    win_ref[halo + tg:, :] = next_ref[...] * (i < n - 1).astype(F32)
    pad = CONV_K // 2
    acc = cw_ref[0:1, :] * win_ref[pl.ds(halo - pad, tg), :]
    for t in range(1, CONV_K):
        acc = acc + cw_ref[t:t + 1, :] * win_ref[pl.ds(halo - pad + t, tg), :]
    act = _silu(acc)
    hs = hs_ref[...]
    q = act[:, 0:GROUP_W]
    k = act[:, GROUP_W:2 * GROUP_W]
    qn = q * lax.rsqrt(_dot_split(q * q, hs) + EPS) * (HEAD_DIM ** -0.5)
    kn = k * lax.rsqrt(_dot_split(k * k, hs) + EPS)
    qkv_ref[:, 0:GROUP_W] = qn
    qkv_ref[:, GROUP_W:2 * GROUP_W] = kn
    qkv_ref[:, 2 * GROUP_W:] = act[:, 2 * GROUP_W:]
    a = gate_ref[...]
    x = a + gp_ref[1:2, :]
    softplus = jnp.maximum(x, 0.0) + jnp.log(1.0 + jnp.exp(-jnp.abs(x)))
    g = -jnp.exp(gp_ref[0:1, :]) * softplus
    gcf = _dot_split_left(tri_ref[0], g)
    gcb = _dot_split_left(tri_ref[1], g)
    lane = lax.broadcasted_iota(jnp.int32, a.shape, 1)
    gb_ref[...] = jnp.where(lane < N_HEADS, gcf, jnp.where(lane < 2 * N_HEADS, gcb, _sigmoid(a)))


def _gdn_pre(bqkv, gates, conv_w, a_log, dt_bias, tg):
    b, l, c3 = bqkv.shape
    halo = SUBLANES
    nt = l // tg
    per = tg // halo
    nh = l // halo
    cw = jnp.zeros((SUBLANES, c3), F32).at[:CONV_K].set(conv_w)
    gp = jnp.zeros((SUBLANES, GATE_PAD), F32)
    gp = gp.at[0, :N_GATE_COLS // 2].set(a_log.reshape(-1)).at[1, :N_GATE_COLS // 2].set(dt_bias.reshape(-1))
    ti = np.arange(tg)
    same_chunk = (ti[:, None] // GDN_CHUNK) == (ti[None, :] // GDN_CHUNK)
    tri = np.stack([same_chunk & (ti[None, :] <= ti[:, None]), same_chunk & (ti[None, :] >= ti[:, None])])
    tri = jnp.asarray(tri.astype(np.float32), dtype=BF16)
    return pl.pallas_call(
        _gdn_pre_kernel,
        grid=(b, nt),
        in_specs=[pl.BlockSpec((None, tg, c3), lambda bi, i: (bi, i, 0)),
                  pl.BlockSpec((None, halo, c3), lambda bi, i: (bi, jnp.maximum(i * per - 1, 0), 0)),
                  pl.BlockSpec((None, halo, c3), lambda bi, i: (bi, jnp.minimum((i + 1) * per, nh - 1), 0)),
                  pl.BlockSpec((None, tg, GATE_PAD), lambda bi, i: (bi, i, 0)),
                  pl.BlockSpec((SUBLANES, c3), lambda bi, i: (0, 0)),
                  pl.BlockSpec((SUBLANES, GATE_PAD), lambda bi, i: (0, 0)),
                  pl.BlockSpec((GROUP_W, GROUP_W), lambda bi, i: (0, 0)),
                  pl.BlockSpec((2, tg, tg), lambda bi, i: (0, 0, 0))],
        out_specs=[pl.BlockSpec((None, tg, c3), lambda bi, i: (bi, i, 0)),
                   pl.BlockSpec((None, tg, GATE_PAD), lambda bi, i: (bi, i, 0))],
        out_shape=[jax.ShapeDtypeStruct((b, l, c3), F32),
                   jax.ShapeDtypeStruct((b, l, GATE_PAD), F32)],
        scratch_shapes=[pltpu.VMEM((tg + 2 * halo, c3), F32)],
        compiler_params=_cparams(("parallel", "parallel")),
        name="gdn_pre",
    )(bqkv, bqkv, bqkv, gates, cw, gp, _head_sum_matrix(), tri)


def _dot_split_left(m_bf16, a):
    hi = a.astype(BF16)
    lo = (a - hi.astype(F32)).astype(BF16)
    return (jnp.dot(m_bf16, hi, preferred_element_type=F32)
            + jnp.dot(m_bf16, lo, preferred_element_type=F32))


def _block_diag(x, bd_mask):
    xb = x.astype(BF16)
    return jnp.where(bd_mask, jnp.concatenate([xb] * N_HEADS, axis=0), jnp.zeros((), BF16))


def _gdn_scan_kernel(xf_ref, gbf_ref, xb_ref, gbb_ref, ex_ref, of_ref, ob_ref, s_ref):
    c = GDN_CHUNK
    ng = xf_ref.shape[0] // c

    @pl.when(pl.program_id(1) == 0)
    def _():
        s_ref[...] = jnp.zeros_like(s_ref)

    row = lax.broadcasted_iota(jnp.int32, (c, GROUP_W), 0)
    lane = lax.broadcasted_iota(jnp.int32, (c, GROUP_W), 1)
    colm = lane & (HEAD_DIM - 1)
    incl = [(row >= colm), (row <= colm)]
    strict = [(row > colm), (row < colm)]
    eye_f = (row == colm).astype(F32)
    bd_mask = ((lax.broadcasted_iota(jnp.int32, (GROUP_W, GROUP_W), 0) >> HEAD_SHIFT)
               == (lax.broadcasted_iota(jnp.int32, (GROUP_W, GROUP_W), 1) >> HEAD_SHIFT))
    bd = lambda a: _block_diag(a, bd_mask)

    streams = [(g, d) for g in range(ng) for d in range(2)]

    def rows_of(g, d):
        lc = g if d == 0 else ng - 1 - g
        return pl.ds(lc * c, c)

    x_refs = (xf_ref, xb_ref)
    gb_refs = (gbf_ref, gbb_ref)
    gb = [gb_refs[d][rows_of(g, d), :] for g, d in streams]
    gb_hi = [a.astype(BF16) for a in gb]
    gb_hl = [jnp.concatenate([hi, (a - hi.astype(F32)).astype(BF16)], axis=1) for a, hi in zip(gb, gb_hi)]
    ex = [jnp.dot(a, ex_ref[d], preferred_element_type=F32) for (g, d), a in zip(streams, gb_hl)]
    gc = [a[:, :GROUP_W] for a in ex]
    beta = [a[:, GROUP_W:] for a in ex]
    gbt = [a.T for a in gb]
    gcr = [jnp.concatenate([t[N_HEADS * d + h:N_HEADS * d + h + 1, :] for h in range(N_HEADS)], axis=1)
           for (g, d), t in zip(streams, gbt)]
    decay = [jnp.exp(jnp.where(incl[d], a - r, -jnp.inf)) for (g, d), a, r in zip(streams, gc, gcr)]
    last = [c - 1, 0]
    g_last = [a[last[d]:last[d] + 1, :] for (g, d), a in zip(streams, gc)]
    e_g = [jnp.exp(a) for a in gc]
    e_rem = [jnp.exp(gl - a) for gl, a in zip(g_last, gc)]
    e_last = [jnp.exp(gl) for gl in g_last]

    q = [x_refs[d][rows_of(g, d), 0:GROUP_W] for g, d in streams]
    k = [x_refs[d][rows_of(g, d), GROUP_W:2 * GROUP_W] for g, d in streams]
    v = [x_refs[d][rows_of(g, d), 2 * GROUP_W:3 * GROUP_W] for g, d in streams]
    kb = [a * b for a, b in zip(k, beta)]
    bdk = [bd(a) for a in k]
    kk = [_dot_nt(a, b) for a, b in zip(kb, bdk)]
    lm = [jnp.where(strict[d], a * dec, 0.0) for (g, d), a, dec in zip(streams, kk, decay)]
    minv = [eye_f - jnp.where((row >> 1) == (colm >> 1), a, 0.0) for a in lm]
    sh = 1
    while (2 << sh) <= c:
        sel = ((row >> (sh + 1)) == (colm >> (sh + 1))) & ((row >> sh) != (colm >> sh))
        t = [_dot(m, bd(jnp.where(sel, a, 0.0))) for m, a in zip(minv, lm)]
        t = [_dot(a, bd(m)) for a, m in zip(t, minv)]
        minv = [m - a for m, a in zip(minv, t)]
        sh += 1
    u = [_dot(m, bd(a * b)) for m, a, b in zip(minv, v, beta)]
    w = [_dot(m, bd(a * e)) for m, a, e in zip(minv, kb, e_g)]
    qk = [_dot_nt(a, b) * dec for a, b, dec in zip(q, bdk, decay)]
    qd = [a * e for a, e in zip(q, e_g)]
    kdt = [(a * e).T for a, e in zip(k, e_rem)]

    o_refs = (of_ref, ob_ref)
    for g in range(ng):
        idx = [2 * g, 2 * g + 1]
        state = [s_ref[d] for d in range(2)]
        bds = [bd(s) for s in state]
        ws = [_dot(jnp.concatenate([w[i], qd[i]], axis=0), b) for i, b in zip(idx, bds)]
        v_new = [u[i] - a[:c] for i, a in zip(idx, ws)]
        o2 = [_dot(qk[i], bd(a)) for i, a in zip(idx, v_new)]
        upd = [_dot(kdt[i], a) for i, a in zip(idx, v_new)]
        for d in range(2):
            i = idx[d]
            o_refs[d][rows_of(g, d), :] = ws[d][c:] + o2[d]
            diag = jnp.where(bd_mask, upd[d], 0.0)
            fold = diag[0:c] + diag[c:2 * c] + diag[2 * c:3 * c] + diag[3 * c:4 * c]
            s_ref[d] = state[d] * e_last[i] + fold


def _gate_expanders():
    e = np.zeros((2, 2 * GATE_PAD, 2 * GROUP_W), np.float32)
    for d in range(2):
        for h in range(N_HEADS):
            for half in range(2):
                r0 = half * GATE_PAD
                e[d, r0 + N_HEADS * d + h, h * HEAD_DIM:(h + 1) * HEAD_DIM] = 1.0
                e[d, r0 + N_GATE_COLS // 2 + N_HEADS * d + h, GROUP_W + h * HEAD_DIM:GROUP_W + (h + 1) * HEAD_DIM] = 1.0
    return jnp.asarray(e, dtype=BF16)


def _gdn_scan(qkvn, gb, ng):
    b, l, c3 = qkvn.shape
    rows = ng * GDN_CHUNK
    nsteps = l // rows
    fwd = lambda w: pl.BlockSpec((None, rows, w), lambda bi, i: (bi, i, 0))
    bwd = lambda w: pl.BlockSpec((None, rows, w), lambda bi, i: (bi, nsteps - 1 - i, 0))
    return pl.pallas_call(
        _gdn_scan_kernel,
        grid=(b, nsteps),
        in_specs=[fwd(c3), fwd(GATE_PAD), bwd(c3), bwd(GATE_PAD),
                  pl.BlockSpec((2, 2 * GATE_PAD, 2 * GROUP_W), lambda bi, i: (0, 0, 0))],
        out_specs=[fwd(GROUP_W), bwd(GROUP_W)],
        out_shape=[jax.ShapeDtypeStruct((b, l, GROUP_W), F32)] * 2,
        scratch_shapes=[pltpu.VMEM((2, HEAD_DIM, GROUP_W), F32)],
        compiler_params=_cparams(("parallel", "arbitrary")),
        name="gdn_scan",
    )(qkvn, gb, qkvn, gb, _gate_expanders())


def _head_stack(q):
    lane_head = lax.broadcasted_iota(jnp.int32, q.shape, 1) >> HEAD_SHIFT
    zero = jnp.zeros((), q.dtype)
    return jnp.concatenate([jnp.where(lane_head == h, q, zero) for h in range(N_HEADS)], axis=0)


def _head_unstack(pv, m):
    lane_head = lax.broadcasted_iota(jnp.int32, (m, GROUP_W), 1) >> HEAD_SHIFT
    out = jnp.where(lane_head == 0, pv[0:m], 0.0)
    for h in range(1, N_HEADS):
        out = out + jnp.where(lane_head == h, pv[h * m:(h + 1) * m], 0.0)
    return out


def _attend(qs, ks, vs, biases):
    s = [_dot_nt(a, b) for a, b in zip(qs, ks)]
    s = [a if b is None else a + b for a, b in zip(s, biases)]
    m = [jnp.max(a, axis=-1, keepdims=True) for a in s]
    p = [jnp.exp(a - b) for a, b in zip(s, m)]
    den = [jnp.sum(a, axis=-1, keepdims=True) for a in p]
    pv = [jnp.dot(a.astype(BF16), b, preferred_element_type=F32) for a, b in zip(p, vs)]
    return [a / b for a, b in zip(pv, den)]


def _natten_kernel(q_ref, kp_ref, kc_ref, kn_ref, vp_ref, vc_ref, vn_ref, bias_ref, o_ref, kwin_ref, vwin_ref):
    gi = pl.program_id(1)
    rows = pl.num_programs(1) * NA_KH
    blk = kc_ref.shape[0]
    for t, (kr, vr) in enumerate(((kp_ref, vp_ref), (kc_ref, vc_ref), (kn_ref, vn_ref))):
        kwin_ref[t * blk:(t + 1) * blk, :] = kr[...]
        vwin_ref[t * blk:(t + 1) * blk, :] = vr[...]

    def body(t, carry):
        js = [t * NA_ROWS_PER_ITER + i for i in range(NA_ROWS_PER_ITER)]
        r = [gi * NA_KH + j for j in js]
        rs = [jnp.clip(a - NA_KH // 2, 0, rows - NA_KH) for a in r]
        start = [pl.multiple_of((a - (gi - 1) * NA_KH) * GRID_W, GRID_W) for a in rs]
        qrows = [pl.ds(pl.multiple_of(j * GRID_W, GRID_W), GRID_W) for j in js]
        qs = [_head_stack(q_ref[a, :] * (HEAD_DIM ** -0.5)) for a in qrows]
        kw = [kwin_ref[pl.ds(a, NA_KH * GRID_W), :] for a in start]
        vw = [vwin_ref[pl.ds(a, NA_KH * GRID_W), :] for a in start]
        bias = [bias_ref[a - b] for a, b in zip(r, rs)]
        for a, pv in zip(qrows, _attend(qs, kw, vw, bias)):
            o_ref[a, :] = _head_unstack(pv, GRID_W)
        return carry

    lax.fori_loop(0, NA_KH // NA_ROWS_PER_ITER, body, 0)


def _natten_bias(rpb):
    w = np.arange(GRID_W)
    cs = np.clip(w - NA_KW // 2, 0, GRID_W - NA_KW)
    wk = np.arange(GRID_W)
    in_win = (wk[None, :] >= cs[:, None]) & (wk[None, :] < cs[:, None] + NA_KW)
    col_off = wk[None, :] - w[:, None] + NA_KW - 1
    delta = np.arange(NA_KH)
    i = np.arange(NA_KH)
    row_off = i[None, :] - delta[:, None] + NA_KH - 1
    sel_r = (row_off[:, :, None] == np.arange(2 * NA_KH - 1)).astype(np.float32)
    sel_c = ((col_off[:, :, None] == np.arange(2 * NA_KW - 1)) & in_win[:, :, None]).astype(np.float32)
    t = jnp.einsum('hab,dia,wvb->hdiwv', rpb.astype(F32), sel_r, sel_c, precision=lax.Precision.HIGHEST)
    t = jnp.where(jnp.asarray(in_win)[None, None, None], t, NEG_BIG)
    t = t.transpose(1, 0, 3, 2, 4)
    return t.reshape(NA_KH, N_HEADS * GRID_W, NA_KH * GRID_W).astype(F32)


def _natten(cqkv, bias):
    b, l, _ = cqkv.shape
    blk = NA_KH * GRID_W
    ng = l // blk
    spec = lambda colblk, f: pl.BlockSpec((None, blk, GROUP_W), lambda bi, g: (bi, f(g), colblk))
    cur = lambda g: g
    prev = lambda g: jnp.maximum(g - 1, 0)
    nxt = lambda g: jnp.minimum(g + 1, ng - 1)
    return pl.pallas_call(
        _natten_kernel,
        grid=(b, ng),
        in_specs=[spec(0, cur), spec(1, prev), spec(1, cur), spec(1, nxt),
                  spec(2, prev), spec(2, cur), spec(2, nxt),
                  pl.BlockSpec(bias.shape, lambda bi, g: (0, 0, 0))],
        out_specs=pl.BlockSpec((None, blk, GROUP_W), lambda bi, g: (bi, g, 0)),
        out_shape=jax.ShapeDtypeStruct((b, l, GROUP_W), F32),
        scratch_shapes=[pltpu.VMEM((3 * blk, GROUP_W), BF16), pltpu.VMEM((3 * blk, GROUP_W), BF16)],
        compiler_params=_cparams(("parallel", "parallel")),
        name="natten",
    )(cqkv, cqkv, cqkv, cqkv, cqkv, cqkv, cqkv, bias)


def _memattn_kernel(q_ref, mem_ref, mg_ref, wkv_ref, o_ref, k_ref, v_ref):
    @pl.when(pl.program_id(1) == 0)
    def _():
        m = mem_ref[...]
        ms = jnp.mean(m * m, axis=-1, keepdims=True)
        mn = (m * lax.rsqrt(ms + EPS) * mg_ref[...]).astype(BF16)
        kv = jnp.dot(mn, wkv_ref[...], preferred_element_type=F32)
        k_ref[...] = kv[:, :GROUP_W].astype(BF16)
        v_ref[...] = kv[:, GROUP_W:].astype(BF16)

    ts = q_ref.shape[0] // MEM_Q_SPLIT
    parts = [pl.ds(i * ts, ts) for i in range(MEM_Q_SPLIT)]
    qs = [_head_stack(q_ref[a, :] * (HEAD_DIM ** -0.5)) for a in parts]
    k = k_ref[...]
    v = v_ref[...]
    for a, pv in zip(parts, _attend(qs, [k] * MEM_Q_SPLIT, [v] * MEM_Q_SPLIT, [None] * MEM_Q_SPLIT)):
        o_ref[a, :] = _head_unstack(pv, ts)


def _memattn(dq, mem, mem_g, w_mem_kv, tq):
    b, l, c = dq.shape
    m, d = mem.shape[1], mem.shape[2]
    return pl.pallas_call(
        _memattn_kernel,
        grid=(b, l // tq),
        in_specs=[pl.BlockSpec((None, tq, c), lambda bi, i: (bi, i, 0)),
                  pl.BlockSpec((None, m, d), lambda bi, i: (bi, 0, 0)),
                  pl.BlockSpec((1, d), lambda bi, i: (0, 0)),
                  pl.BlockSpec((d, 2 * c), lambda bi, i: (0, 0))],
        out_specs=pl.BlockSpec((None, tq, c), lambda bi, i: (bi, i, 0)),
        out_shape=jax.ShapeDtypeStruct((b, l, c), F32),
        scratch_shapes=[pltpu.VMEM((m, c), BF16), pltpu.VMEM((m, c), BF16)],
        compiler_params=_cparams(("parallel", "arbitrary")),
        name="memattn",
    )(dq, mem, mem_g.reshape(1, d), w_mem_kv.astype(BF16))


def _outproj_kernel(ya_ref, of_ref, ob_ref, yc_ref, yd_ref, z_ref, x_ref, gg_ref, pg_ref, hs_ref, w_ref, o_ref):
    o = of_ref[...] + ob_ref[...]
    ms = _dot_split(o * o, hs_ref[...]) * (1.0 / HEAD_DIM)
    yb = o * lax.rsqrt(ms + EPS) * gg_ref[...]
    acc = None
    for i, y in enumerate((ya_ref[...], yb, yc_ref[...], yd_ref[...])):
        gated = (y * _silu(z_ref[:, i * GROUP_W:(i + 1) * GROUP_W])).astype(BF16)
        part = jnp.dot(gated, w_ref[i * GROUP_W:(i + 1) * GROUP_W, :], preferred_element_type=F32)
        acc = part if acc is None else acc + part
    ms = jnp.mean(acc * acc, axis=-1, keepdims=True)
    o_ref[...] = x_ref[...] + acc * lax.rsqrt(ms + EPS) * pg_ref[...]


def _outproj(ya, o_f, o_b, yc, yd, z, x, gdn_g, post_g, w_out, tm):
    b, l, d = x.shape
    grp = pl.BlockSpec((None, tm, GROUP_W), lambda bi, i: (bi, i, 0))
    full = pl.BlockSpec((None, tm, d), lambda bi, i: (bi, i, 0))
    const = lambda shape: pl.BlockSpec(shape, lambda bi, i: (0, 0))
    return pl.pallas_call(
        _outproj_kernel,
        grid=(b, l // tm),
        in_specs=[grp, grp, grp, grp, grp, full, full,
                  const((1, GROUP_W)), const((1, d)), const((GROUP_W, GROUP_W)), const((d, d))],
        out_specs=full,
        out_shape=jax.ShapeDtypeStruct((b, l, d), F32),
        compiler_params=_cparams(("parallel", "parallel")),
        name="outproj",
    )(ya, o_f, o_b, yc, yd, z, x, jnp.tile(gdn_g, N_HEADS).reshape(1, GROUP_W), post_g.reshape(1, d),
      _head_sum_matrix(), w_out.astype(BF16))


def _tile(l, want):
    t = min(want, l)
    assert l % t == 0
    return t


def _layer(x, mem, tables, pre_g, post_g, w_perm, w_fnet, conv_w, a_log, dt_bias, gdn_g, na_bias, mem_g, w_mem_kv,
           w_out):
    l = x.shape[1]
    u, bqkv, bgate, cqkv, dq, z = _inproj(x, pre_g, w_perm, _tile(l, 512))
    ya = _fnet(u, w_fnet, tables)
    qkvn, gb = _gdn_pre(bqkv, bgate, conv_w, a_log, dt_bias, _tile(l, 512))
    o_f, o_b = _gdn_scan(qkvn, gb, GDN_CHUNKS_PER_STEP)
    yc = _natten(cqkv, na_bias)
    yd = _memattn(dq, mem, mem_g, w_mem_kv, _tile(l, 512))
    return _outproj(ya, o_f, o_b, yc, yd, z, x, gdn_g, post_g, w_out, _tile(l, 512))


def kernel(x_prompt, x_sample, mem_prompt, mem_sample, pre_norm_g, post_norm_g, w_in, w_fnet, gdn_conv_w,
           gdn_a_log, gdn_dt_bias, gdn_norm_g, na_rpb, mem_norm_g, w_mem_kv, w_out):
    l = x_prompt.shape[1]
    assert x_prompt.shape[1:] == x_sample.shape[1:] == (l, D_MODEL)
    assert l % (NA_KH * GRID_W) == 0 and l % (GDN_CHUNKS_PER_STEP * GDN_CHUNK) == 0 and l % FFT_NA == 0
    depth = pre_norm_g.shape[0]
    tables = _fnet_tables(l)
    w_perm = [_permute_w_in(w_in[i]) for i in range(depth)]
    na_bias = [_natten_bias(na_rpb[i]) for i in range(depth)]

    def trunk(x, mem):
        for i in range(depth):
            x = _layer(x, mem, tables, pre_norm_g[i], post_norm_g[i], w_perm[i], w_fnet[i], gdn_conv_w[i],
                       gdn_a_log[i], gdn_dt_bias[i], gdn_norm_g[i], na_bias[i], mem_norm_g[i], w_mem_kv[i], w_out[i])
        return x

    return (trunk(x_prompt, mem_prompt), trunk(x_sample, mem_sample))
```

```python
import functools
import math

import numpy as np
import jax
import jax.numpy as jnp
from jax import lax
from jax.experimental import pallas as pl
from jax.experimental.pallas import tpu as pltpu

F32 = jnp.float32
BF16 = jnp.bfloat16

D_MODEL = 1024
GROUP_W = 256
HEAD_DIM = 64
HEAD_SHIFT = 6
N_HEADS = 4
FNET_BLOCK_W = 64
GDN_CHUNK = 64
CONV_K = 5
GRID_W = 64
NA_KH = 8
NA_KW = 16
N_GATE_COLS = 16
EPS = 1e-6
NEG_BIG = -1e30

LANES = 128
SUBLANES = 8
GATE_PAD = LANES
D_PROJ = GROUP_W + 3 * GROUP_W + GATE_PAD + 3 * GROUP_W + GROUP_W + 4 * GROUP_W
FFT_NA = 64
FNET_KA_PER_STEP = 8
GDN_CHUNKS_PER_STEP = 4
NA_ROWS_PER_ITER = 2
MEM_Q_SPLIT = 2
VMEM_LIMIT = 56 * 1024 * 1024


def _cparams(sem):
    return pltpu.CompilerParams(dimension_semantics=sem, vmem_limit_bytes=VMEM_LIMIT)


def _dot(a, b):
    return jnp.dot(a.astype(BF16), b.astype(BF16), preferred_element_type=F32)


def _dot_nt(a, b):
    return lax.dot_general(a.astype(BF16), b.astype(BF16), (((1,), (1,)), ((), ())),
                           preferred_element_type=F32)


def _dot_split(a, b_bf16):
    hi = a.astype(BF16)
    lo = (a - hi.astype(F32)).astype(BF16)
    return (jnp.dot(hi, b_bf16, preferred_element_type=F32)
            + jnp.dot(lo, b_bf16, preferred_element_type=F32))


def _sigmoid(x):
    return 1.0 / (1.0 + jnp.exp(-x))


def _silu(x):
    return x * _sigmoid(x)


def _head_sum_matrix():
    idx = np.arange(GROUP_W) // HEAD_DIM
    return jnp.asarray((idx[:, None] == idx[None, :]).astype(np.float32), dtype=BF16)


def _inproj_kernel(x_ref, g_ref, w_ref, u_ref, bqkv_ref, bg_ref, cqkv_ref, dq_ref, z_ref):
    x = x_ref[...]
    ms = jnp.mean(x * x, axis=-1, keepdims=True)
    h = (x * lax.rsqrt(ms + EPS) * g_ref[...]).astype(BF16)
    off = 0
    for ref in (u_ref, bqkv_ref, bg_ref, cqkv_ref, dq_ref, z_ref):
        n = ref.shape[-1]
        ref[...] = jnp.dot(h, w_ref[:, off:off + n], preferred_element_type=F32).astype(ref.dtype)
        off += n


def _permute_w_in(w_in):
    g = GROUP_W
    o_az, o_bqkv, o_bz, o_bg = g, 2 * g, 5 * g, 6 * g
    o_cqkv = o_bg + N_GATE_COLS
    o_cz, o_dq, o_dz = o_cqkv + 3 * g, o_cqkv + 4 * g, o_cqkv + 5 * g
    pad = jnp.zeros((D_MODEL, GATE_PAD - N_GATE_COLS), w_in.dtype)
    cols = [w_in[:, 0:g], w_in[:, o_bqkv:o_bz], w_in[:, o_bg:o_cqkv], pad, w_in[:, o_cqkv:o_cz],
            w_in[:, o_dq:o_dz], w_in[:, o_az:o_bqkv], w_in[:, o_bz:o_bg], w_in[:, o_cz:o_dq],
            w_in[:, o_dz:o_dz + g]]
    return jnp.concatenate(cols, axis=1).astype(BF16)


def _inproj(x, pre_g, w_perm, tm):
    b, l, d = x.shape
    widths = (GROUP_W, 3 * GROUP_W, GATE_PAD, 3 * GROUP_W, GROUP_W, 4 * GROUP_W)
    dtypes = (BF16, BF16, F32, BF16, BF16, BF16)
    row = lambda n: pl.BlockSpec((None, tm, n), lambda bi, i: (bi, i, 0))
    return pl.pallas_call(
        _inproj_kernel,
        grid=(b, l // tm),
        in_specs=[row(d),
                  pl.BlockSpec((1, d), lambda bi, i: (0, 0)),
                  pl.BlockSpec((d, D_PROJ), lambda bi, i: (0, 0))],
        out_specs=[row(n) for n in widths],
        out_shape=[jax.ShapeDtypeStruct((b, l, n), dt) for n, dt in zip(widths, dtypes)],
        compiler_params=_cparams(("parallel", "parallel")),
        name="inproj",
    )(x, pre_g.reshape(1, d), w_perm)


def _fnet_a_kernel(x_ref, t_ref, y_ref):
    y_ref[...] = jnp.dot(t_ref[...], x_ref[...].astype(BF16), preferred_element_type=F32).astype(BF16)


def _fnet_b_kernel(y_ref, g_ref, bcs_ref, wf_ref, o_ref):
    nk, nb = o_ref.shape[0], o_ref.shape[1]
    bc = bcs_ref[:GROUP_W, :]
    bs = bcs_ref[GROUP_W:, :]
    wf = wf_ref[...]
    z = [jnp.dot(g_ref[i], y_ref[i], preferred_element_type=F32) for i in range(nk)]
    f = [jnp.dot(a[:nb].astype(BF16), bc, preferred_element_type=F32)
         + jnp.dot(a[nb:].astype(BF16), bs, preferred_element_type=F32) for a in z]
    for i, a in enumerate(f):
        o_ref[i] = jnp.dot(a.astype(BF16), wf, preferred_element_type=F32).astype(o_ref.dtype)


def _fnet_tables(l):
    na, nb = FFT_NA, l // FFT_NA
    ia = np.arange(na)
    ang_a = 2.0 * np.pi * ((ia[:, None] * ia[None, :]) % na) / na
    ta = np.concatenate([np.cos(ang_a), -np.sin(ang_a)], axis=0) / math.sqrt(na)
    ka = jnp.arange(na, dtype=jnp.int32)[:, None, None]
    kb = jnp.arange(nb, dtype=jnp.int32)[None, :, None]
    n2 = jnp.arange(nb, dtype=jnp.int32)[None, None, :]
    ang = (2.0 * math.pi / l) * ((n2 * (ka + na * kb)) % l).astype(F32)
    cg = jnp.cos(ang) / math.sqrt(nb)
    sg = jnp.sin(ang) / math.sqrt(nb)
    gbig = jnp.concatenate([jnp.concatenate([cg, sg], axis=2),
                            jnp.concatenate([-sg, cg], axis=2)], axis=1).astype(BF16)
    ic = np.arange(GROUP_W)
    same = (ic[:, None] // FNET_BLOCK_W) == (ic[None, :] // FNET_BLOCK_W)
    ang_c = 2.0 * np.pi * (((ic[:, None] % FNET_BLOCK_W) * (ic[None, :] % FNET_BLOCK_W)) % FNET_BLOCK_W) / FNET_BLOCK_W
    bc = np.where(same, np.cos(ang_c), 0.0) / math.sqrt(FNET_BLOCK_W)
    bs = np.where(same, np.sin(ang_c), 0.0) / math.sqrt(FNET_BLOCK_W)
    bcs = np.concatenate([bc, bs], axis=0)
    return jnp.asarray(ta, dtype=F32).astype(BF16), gbig, jnp.asarray(bcs, dtype=F32).astype(BF16)


def _fnet(u, w_fnet, tables):
    b, l, c = u.shape
    na, nb = FFT_NA, l // FFT_NA
    ta, gbig, bcs = tables
    cw = min(2048, nb * c)
    x2 = u.reshape(b, na, nb * c)
    y = pl.pallas_call(
        _fnet_a_kernel,
        grid=(b, nb * c // cw),
        in_specs=[pl.BlockSpec((None, na, cw), lambda bi, j: (bi, 0, j)),
                  pl.BlockSpec((2 * na, na), lambda bi, j: (0, 0))],
        out_specs=pl.BlockSpec((None, 2 * na, cw), lambda bi, j: (bi, 0, j)),
        out_shape=jax.ShapeDtypeStruct((b, 2 * na, nb * c), BF16),
        compiler_params=_cparams(("parallel", "parallel")),
        name="fnet_a",
    )(x2, ta)
    ys = y.reshape(b, 2, na, nb, c).transpose(0, 2, 1, 3, 4).reshape(b, na, 2 * nb, c)
    o = pl.pallas_call(
        _fnet_b_kernel,
        grid=(b, na // FNET_KA_PER_STEP),
        in_specs=[pl.BlockSpec((None, FNET_KA_PER_STEP, 2 * nb, c), lambda bi, k: (bi, k, 0, 0)),
                  pl.BlockSpec((FNET_KA_PER_STEP, 2 * nb, 2 * nb), lambda bi, k: (k, 0, 0)),
                  pl.BlockSpec((2 * c, c), lambda bi, k: (0, 0)),
                  pl.BlockSpec((c, c), lambda bi, k: (0, 0))],
        out_specs=pl.BlockSpec((None, FNET_KA_PER_STEP, nb, c), lambda bi, k: (bi, k, 0, 0)),
        out_shape=jax.ShapeDtypeStruct((b, na, nb, c), BF16),
        compiler_params=_cparams(("parallel", "parallel")),
        name="fnet_b",
    )(ys, gbig, bcs, w_fnet.astype(BF16))
    return o.transpose(0, 2, 1, 3).reshape(b, l, c)


def _gdn_pre_kernel(cur_ref, prev_ref, next_ref, gate_ref, cw_ref, gp_ref, hs_ref, tri_ref, qkv_ref, gb_ref, win_ref):
    i = pl.program_id(1)
    n = pl.num_programs(1)
    tg = cur_ref.shape[0]
    halo = prev_ref.shape[0]
    win_ref[0:halo, :] = prev_ref[...].astype(F32) * (i > 0).astype(F32)
    win_ref[halo:halo + tg, :] = cur_ref[...].astype(F32)
    win_ref[halo + tg:, :] = next_ref[...].astype(F32) * (i < n - 1).astype(F32)
    pad = CONV_K // 2
    acc = cw_ref[0:1, :] * win_ref[pl.ds(halo - pad, tg), :]
    for t in range(1, CONV_K):
        acc = acc + cw_ref[t:t + 1, :] * win_ref[pl.ds(halo - pad + t, tg), :]
    act = _silu(acc)
    hs = hs_ref[...]
    q = act[:, 0:GROUP_W]
    k = act[:, GROUP_W:2 * GROUP_W]
    qn = q * lax.rsqrt(_dot_split(q * q, hs) + EPS) * (HEAD_DIM ** -0.5)
    kn = k * lax.rsqrt(_dot_split(k * k, hs) + EPS)
    qkv_ref[:, 0:GROUP_W] = qn.astype(qkv_ref.dtype)
    qkv_ref[:, GROUP_W:2 * GROUP_W] = kn.astype(qkv_ref.dtype)
    qkv_ref[:, 2 * GROUP_W:] = act[:, 2 * GROUP_W:].astype(qkv_ref.dtype)
    a = gate_ref[...]
    x = a + gp_ref[1:2, :]
    softplus = jnp.maximum(x, 0.0) + jnp.log(1.0 + jnp.exp(-jnp.abs(x)))
    g = -jnp.exp(gp_ref[0:1, :]) * softplus
    gcf = _dot_split_left(tri_ref[0], g)
    gcb = _dot_split_left(tri_ref[1], g)
    lane = lax.broadcasted_iota(jnp.int32, a.shape, 1)
    gb_ref[...] = jnp.where(lane < N_HEADS, gcf, jnp.where(lane < 2 * N_HEADS, gcb, _sigmoid(a)))


def _gdn_pre(bqkv, gates, conv_w, a_log, dt_bias, tg):
    b, l, c3 = bqkv.shape
    halo = 2 * SUBLANES
    nt = l // tg
    per = tg // halo
    nh = l // halo
    cw = jnp.zeros((SUBLANES, c3), F32).at[:CONV_K].set(conv_w)
    gp = jnp.zeros((SUBLANES, GATE_PAD), F32)
    gp = gp.at[0, :N_GATE_COLS // 2].set(a_log.reshape(-1)).at[1, :N_GATE_COLS // 2].set(dt_bias.reshape(-1))
    ti = np.arange(tg)
    same_chunk = (ti[:, None] // GDN_CHUNK) == (ti[None, :] // GDN_CHUNK)
    tri = np.stack([same_chunk & (ti[None, :] <= ti[:, None]), same_chunk & (ti[None, :] >= ti[:, None])])
    tri = jnp.asarray(tri.astype(np.float32), dtype=BF16)
    return pl.pallas_call(
        _gdn_pre_kernel,
        grid=(b, nt),
        in_specs=[pl.BlockSpec((None, tg, c3), lambda bi, i: (bi, i, 0)),
                  pl.BlockSpec((None, halo, c3), lambda bi, i: (bi, jnp.maximum(i * per - 1, 0), 0)),
                  pl.BlockSpec((None, halo, c3), lambda bi, i: (bi, jnp.minimum((i + 1) * per, nh - 1), 0)),
                  pl.BlockSpec((None, tg, GATE_PAD), lambda bi, i: (bi, i, 0)),
                  pl.BlockSpec((SUBLANES, c3), lambda bi, i: (0, 0)),
                  pl.BlockSpec((SUBLANES, GATE_PAD), lambda bi, i: (0, 0)),
                  pl.BlockSpec((GROUP_W, GROUP_W), lambda bi, i: (0, 0)),
                  pl.BlockSpec((2, tg, tg), lambda bi, i: (0, 0, 0))],
        out_specs=[pl.BlockSpec((None, tg, c3), lambda bi, i: (bi, i, 0)),
                   pl.BlockSpec((None, tg, GATE_PAD), lambda bi, i: (bi, i, 0))],
        out_shape=[jax.ShapeDtypeStruct((b, l, c3), BF16),
                   jax.ShapeDtypeStruct((b, l, GATE_PAD), F32)],
        scratch_shapes=[pltpu.VMEM((tg + 2 * halo, c3), F32)],
        compiler_params=_cparams(("parallel", "parallel")),
        name="gdn_pre",
    )(bqkv, bqkv, bqkv, gates, cw, gp, _head_sum_matrix(), tri)


def _dot_split_left(m_bf16, a):
    hi = a.astype(BF16)
    lo = (a - hi.astype(F32)).astype(BF16)
    return (jnp.dot(m_bf16, hi, preferred_element_type=F32)
            + jnp.dot(m_bf16, lo, preferred_element_type=F32))


def _block_diag(x, bd_mask):
    xb = x.astype(BF16)
    return jnp.where(bd_mask, jnp.concatenate([xb] * N_HEADS, axis=0), jnp.zeros((), BF16))


def _gdn_scan_kernel(xf_ref, gbf_ref, xb_ref, gbb_ref, ex_ref, of_ref, ob_ref,
                     s_ref, u_ref, wq_ref, qk_ref, kdt_ref, el_ref):
    c = GDN_CHUNK
    ng = xf_ref.shape[0] // c
    step = pl.program_id(1)
    slot_w = step % 2
    slot_r = 1 - slot_w

    @pl.when(step == 0)
    def _():
        for ref in (s_ref, u_ref, wq_ref, qk_ref, kdt_ref, el_ref):
            ref[...] = jnp.zeros_like(ref)

    row = lax.broadcasted_iota(jnp.int32, (c, GROUP_W), 0)
    lane = lax.broadcasted_iota(jnp.int32, (c, GROUP_W), 1)
    colm = lane & (HEAD_DIM - 1)
    incl = [(row >= colm), (row <= colm)]
    strict = [(row > colm), (row < colm)]
    eye_f = (row == colm).astype(F32)
    bd_mask = ((lax.broadcasted_iota(jnp.int32, (GROUP_W, GROUP_W), 0) >> HEAD_SHIFT)
               == (lax.broadcasted_iota(jnp.int32, (GROUP_W, GROUP_W), 1) >> HEAD_SHIFT))
    bd = lambda a: _block_diag(a, bd_mask)

    streams = [(g, d) for g in range(ng) for d in range(2)]

    def rows_of(g, d):
        lc = g if d == 0 else ng - 1 - g
        return pl.ds(lc * c, c)

    x_refs = (xf_ref, xb_ref)
    gb_refs = (gbf_ref, gbb_ref)
    o_refs = (of_ref, ob_ref)

    def pre_phase():
        gb = [gb_refs[d][rows_of(g, d), :] for g, d in streams]
        gb_hi = [a.astype(BF16) for a in gb]
        gb_hl = [jnp.concatenate([hi, (a - hi.astype(F32)).astype(BF16)], axis=1) for a, hi in zip(gb, gb_hi)]
        ex = [jnp.dot(a, ex_ref[d], preferred_element_type=F32) for (g, d), a in zip(streams, gb_hl)]
        gc = [a[:, :GROUP_W] for a in ex]
        beta = [a[:, GROUP_W:] for a in ex]
        yield
        gbt = [a.T for a in gb]
        gcr = [jnp.concatenate([t[N_HEADS * d + h:N_HEADS * d + h + 1, :] for h in range(N_HEADS)], axis=1)
               for (g, d), t in zip(streams, gbt)]
        decay = [jnp.exp(jnp.where(incl[d], a - r, -jnp.inf)) for (g, d), a, r in zip(streams, gc, gcr)]
        last = [c - 1, 0]
        g_last = [a[last[d]:last[d] + 1, :] for (g, d), a in zip(streams, gc)]
        e_g = [jnp.exp(a) for a in gc]
        e_rem = [jnp.exp(gl - a) for gl, a in zip(g_last, gc)]
        for i, gl in enumerate(g_last):
            el_ref[slot_w, i] = jnp.broadcast_to(jnp.exp(gl), (SUBLANES, GROUP_W))
        yield
        q = [x_refs[d][rows_of(g, d), 0:GROUP_W] for g, d in streams]
        k = [x_refs[d][rows_of(g, d), GROUP_W:2 * GROUP_W] for g, d in streams]
        v = [x_refs[d][rows_of(g, d), 2 * GROUP_W:3 * GROUP_W] for g, d in streams]
        kb = [a * b for a, b in zip(k, beta)]
        kq = [_dot_nt(jnp.concatenate([a, b], axis=0), bd(kk)) for a, b, kk in zip(kb, q, k)]
        lm = [jnp.where(strict[d], a[:c] * dec, 0.0) for (g, d), a, dec in zip(streams, kq, decay)]
        yield
        for i, (a, dec) in enumerate(zip(kq, decay)):
            qk_ref[slot_w, i] = (a[c:] * dec).astype(BF16)
        for i, (a, e) in enumerate(zip(k, e_rem)):
            kdt_ref[slot_w, i] = (a * e).T.astype(BF16)
        minv = [eye_f - jnp.where((row >> 1) == (colm >> 1), a, 0.0) for a in lm]
        sh = 1
        while (2 << sh) <= c:
            sel = ((row >> (sh + 1)) == (colm >> (sh + 1))) & ((row >> sh) != (colm >> sh))
            t = [_dot(m, bd(jnp.where(sel, a, 0.0))) for m, a in zip(minv, lm)]
            yield
            t = [_dot(a, bd(m)) for a, m in zip(t, minv)]
            minv = [m - a for m, a in zip(minv, t)]
            yield
            sh += 1
        for i, (m, a, b) in enumerate(zip(minv, v, beta)):
            u_ref[slot_w, i] = _dot(m, bd(a * b))
        yield
        for i, (m, a, e, qq) in enumerate(zip(minv, kb, e_g, q)):
            wq_ref[slot_w, i] = jnp.concatenate([_dot(m, bd(a * e)), qq * e], axis=0).astype(BF16)

    def state_phase():
        for g in range(ng):
            idx = [2 * g, 2 * g + 1]
            state = [s_ref[d] for d in range(2)]
            ws = [jnp.dot(wq_ref[slot_r, i], bd(s), preferred_element_type=F32) for i, s in zip(idx, state)]
            yield
            v_new = [u_ref[slot_r, i] - a[:c] for i, a in zip(idx, ws)]
            o2 = [jnp.dot(qk_ref[slot_r, i], bd(a), preferred_element_type=F32) for i, a in zip(idx, v_new)]
            upd = [jnp.dot(kdt_ref[slot_r, i], a.astype(BF16), preferred_element_type=F32)
                   for i, a in zip(idx, v_new)]
            for d in range(2):
                o_refs[d][rows_of(g, d), :] = ws[d][c:] + o2[d]
                diag = jnp.where(bd_mask, upd[d], 0.0)
                fold = diag[0:c] + diag[c:2 * c] + diag[2 * c:3 * c] + diag[3 * c:4 * c]
                s_ref[d] = state[d] * el_ref[slot_r, idx[d], 0:1, :] + fold
            yield

    phases = [pre_phase(), state_phase()]
    while phases:
        for p in list(phases):
            if next(p, StopIteration) is StopIteration:
                phases.remove(p)


def _gate_expanders():
    e = np.zeros((2, 2 * GATE_PAD, 2 * GROUP_W), np.float32)
    for d in range(2):
        for h in range(N_HEADS):
            for half in range(2):
                r0 = half * GATE_PAD
                e[d, r0 + N_HEADS * d + h, h * HEAD_DIM:(h + 1) * HEAD_DIM] = 1.0
                e[d, r0 + N_GATE_COLS // 2 + N_HEADS * d + h, GROUP_W + h * HEAD_DIM:GROUP_W + (h + 1) * HEAD_DIM] = 1.0
    return jnp.asarray(e, dtype=BF16)


def _gdn_scan(qkvn, gb, ng):
    b, l, c3 = qkvn.shape
    rows = ng * GDN_CHUNK
    nsteps = l // rows
    grp_in = lambda i: jnp.minimum(i, nsteps - 1)
    grp_out = lambda i: jnp.maximum(i - 1, 0)
    fwd = lambda w, f: pl.BlockSpec((None, rows, w), lambda bi, i: (bi, f(i), 0))
    bwd = lambda w, f: pl.BlockSpec((None, rows, w), lambda bi, i: (bi, nsteps - 1 - f(i), 0))
    ns = 2 * ng
    c = GDN_CHUNK
    return pl.pallas_call(
        _gdn_scan_kernel,
        grid=(b, nsteps + 1),
        in_specs=[fwd(c3, grp_in), fwd(GATE_PAD, grp_in), bwd(c3, grp_in), bwd(GATE_PAD, grp_in),
                  pl.BlockSpec((2, 2 * GATE_PAD, 2 * GROUP_W), lambda bi, i: (0, 0, 0))],
        out_specs=[fwd(GROUP_W, grp_out), bwd(GROUP_W, grp_out)],
        out_shape=[jax.ShapeDtypeStruct((b, l, GROUP_W), F32)] * 2,
        scratch_shapes=[pltpu.VMEM((2, HEAD_DIM, GROUP_W), F32),
                        pltpu.VMEM((2, ns, c, GROUP_W), F32),
                        pltpu.VMEM((2, ns, 2 * c, GROUP_W), BF16),
                        pltpu.VMEM((2, ns, c, GROUP_W), BF16),
                        pltpu.VMEM((2, ns, GROUP_W, c), BF16),
                        pltpu.VMEM((2, ns, SUBLANES, GROUP_W), F32)],
        compiler_params=_cparams(("parallel", "arbitrary")),
        name="gdn_scan",
    )(qkvn, gb, qkvn, gb, _gate_expanders())


def _head_stack(q):
    lane_head = lax.broadcasted_iota(jnp.int32, q.shape, 1) >> HEAD_SHIFT
    zero = jnp.zeros((), q.dtype)
    return jnp.concatenate([jnp.where(lane_head == h, q, zero) for h in range(N_HEADS)], axis=0)


def _head_unstack(pv, m):
    lane_head = lax.broadcasted_iota(jnp.int32, (m, GROUP_W), 1) >> HEAD_SHIFT
    out = jnp.where(lane_head == 0, pv[0:m], 0.0)
    for h in range(1, N_HEADS):
        out = out + jnp.where(lane_head == h, pv[h * m:(h + 1) * m], 0.0)
    return out


def _attend(qs, ks, vs, biases):
    s = [_dot_nt(a, b) for a, b in zip(qs, ks)]
    s = [a if b is None else a + b for a, b in zip(s, biases)]
    m = [jnp.max(a, axis=-1, keepdims=True) for a in s]
    p = [jnp.exp(a - b) for a, b in zip(s, m)]
    den = [jnp.sum(a, axis=-1, keepdims=True) for a in p]
    pv = [jnp.dot(a.astype(BF16), b, preferred_element_type=F32) for a, b in zip(p, vs)]
    return [a / b for a, b in zip(pv, den)]


def _natten_kernel(q_ref, kp_ref, kc_ref, kn_ref, vp_ref, vc_ref, vn_ref, bias_ref, o_ref, kwin_ref, vwin_ref):
    gi = pl.program_id(1)
    rows = pl.num_programs(1) * NA_KH
    blk = kc_ref.shape[0]
    for t, (kr, vr) in enumerate(((kp_ref, vp_ref), (kc_ref, vc_ref), (kn_ref, vn_ref))):
        kwin_ref[t * blk:(t + 1) * blk, :] = kr[...]
        vwin_ref[t * blk:(t + 1) * blk, :] = vr[...]

    def body(t, carry):
        js = [t * NA_ROWS_PER_ITER + i for i in range(NA_ROWS_PER_ITER)]
        r = [gi * NA_KH + j for j in js]
        rs = [jnp.clip(a - NA_KH // 2, 0, rows - NA_KH) for a in r]
        start = [pl.multiple_of((a - (gi - 1) * NA_KH) * GRID_W, GRID_W) for a in rs]
        qrows = [pl.ds(pl.multiple_of(j * GRID_W, GRID_W), GRID_W) for j in js]
        qs = [_head_stack(q_ref[a, :] * (HEAD_DIM ** -0.5)) for a in qrows]
        kw = [kwin_ref[pl.ds(a, NA_KH * GRID_W), :] for a in start]
        vw = [vwin_ref[pl.ds(a, NA_KH * GRID_W), :] for a in start]
        bias = [bias_ref[a - b] for a, b in zip(r, rs)]
        for a, pv in zip(qrows, _attend(qs, kw, vw, bias)):
            o_ref[a, :] = _head_unstack(pv, GRID_W).astype(o_ref.dtype)
        return carry

    lax.fori_loop(0, NA_KH // NA_ROWS_PER_ITER, body, 0)


def _natten_bias(rpb):
    w = np.arange(GRID_W)
    cs = np.clip(w - NA_KW // 2, 0, GRID_W - NA_KW)
    wk = np.arange(GRID_W)
    in_win = (wk[None, :] >= cs[:, None]) & (wk[None, :] < cs[:, None] + NA_KW)
    col_off = wk[None, :] - w[:, None] + NA_KW - 1
    delta = np.arange(NA_KH)
    i = np.arange(NA_KH)
    row_off = i[None, :] - delta[:, None] + NA_KH - 1
    sel_r = (row_off[:, :, None] == np.arange(2 * NA_KH - 1)).astype(np.float32)
    sel_c = ((col_off[:, :, None] == np.arange(2 * NA_KW - 1)) & in_win[:, :, None]).astype(np.float32)
    t = jnp.einsum('hab,dia,wvb->hdiwv', rpb.astype(F32), sel_r, sel_c, precision=lax.Precision.HIGHEST)
    t = jnp.where(jnp.asarray(in_win)[None, None, None], t, NEG_BIG)
    t = t.transpose(1, 0, 3, 2, 4)
    return t.reshape(NA_KH, N_HEADS * GRID_W, NA_KH * GRID_W).astype(F32)


def _natten(cqkv, bias):
    b, l, _ = cqkv.shape
    blk = NA_KH * GRID_W
    ng = l // blk
    spec = lambda colblk, f: pl.BlockSpec((None, blk, GROUP_W), lambda bi, g: (bi, f(g), colblk))
    cur = lambda g: g
    prev = lambda g: jnp.maximum(g - 1, 0)
    nxt = lambda g: jnp.minimum(g + 1, ng - 1)
    return pl.pallas_call(
        _natten_kernel,
        grid=(b, ng),
        in_specs=[spec(0, cur), spec(1, prev), spec(1, cur), spec(1, nxt),
                  spec(2, prev), spec(2, cur), spec(2, nxt),
                  pl.BlockSpec(bias.shape, lambda bi, g: (0, 0, 0))],
        out_specs=pl.BlockSpec((None, blk, GROUP_W), lambda bi, g: (bi, g, 0)),
        out_shape=jax.ShapeDtypeStruct((b, l, GROUP_W), BF16),
        scratch_shapes=[pltpu.VMEM((3 * blk, GROUP_W), BF16), pltpu.VMEM((3 * blk, GROUP_W), BF16)],
        compiler_params=_cparams(("parallel", "parallel")),
        name="natten",
    )(cqkv, cqkv, cqkv, cqkv, cqkv, cqkv, cqkv, bias)


def _memattn_kernel(q_ref, mem_ref, mg_ref, wkv_ref, o_ref, k_ref, v_ref):
    @pl.when(pl.program_id(1) == 0)
    def _():
        m = mem_ref[...]
        ms = jnp.mean(m * m, axis=-1, keepdims=True)
        mn = (m * lax.rsqrt(ms + EPS) * mg_ref[...]).astype(BF16)
        kv = jnp.dot(mn, wkv_ref[...], preferred_element_type=F32)
        k_ref[...] = kv[:, :GROUP_W].astype(BF16)
        v_ref[...] = kv[:, GROUP_W:].astype(BF16)

    ts = q_ref.shape[0] // MEM_Q_SPLIT
    parts = [pl.ds(i * ts, ts) for i in range(MEM_Q_SPLIT)]
    qs = [_head_stack(q_ref[a, :] * (HEAD_DIM ** -0.5)) for a in parts]
    k = k_ref[...]
    v = v_ref[...]
    for a, pv in zip(parts, _attend(qs, [k] * MEM_Q_SPLIT, [v] * MEM_Q_SPLIT, [None] * MEM_Q_SPLIT)):
        o_ref[a, :] = _head_unstack(pv, ts).astype(o_ref.dtype)


def _memattn(dq, mem, mem_g, w_mem_kv, tq):
    b, l, c = dq.shape
    m, d = mem.shape[1], mem.shape[2]
    return pl.pallas_call(
        _memattn_kernel,
        grid=(b, l // tq),
        in_specs=[pl.BlockSpec((None, tq, c), lambda bi, i: (bi, i, 0)),
                  pl.BlockSpec((None, m, d), lambda bi, i: (bi, 0, 0)),
                  pl.BlockSpec((1, d), lambda bi, i: (0, 0)),
                  pl.BlockSpec((d, 2 * c), lambda bi, i: (0, 0))],
        out_specs=pl.BlockSpec((None, tq, c), lambda bi, i: (bi, i, 0)),
        out_shape=jax.ShapeDtypeStruct((b, l, c), BF16),
        scratch_shapes=[pltpu.VMEM((m, c), BF16), pltpu.VMEM((m, c), BF16)],
        compiler_params=_cparams(("parallel", "arbitrary")),
        name="memattn",
    )(dq, mem, mem_g.reshape(1, d), w_mem_kv.astype(BF16))


def _outproj_kernel(ya_ref, of_ref, ob_ref, yc_ref, yd_ref, z_ref, x_ref, gg_ref, pg_ref, hs_ref, w_ref, o_ref):
    o = of_ref[...] + ob_ref[...]
    ms = _dot_split(o * o, hs_ref[...]) * (1.0 / HEAD_DIM)
    yb = o * lax.rsqrt(ms + EPS) * gg_ref[...]
    acc = None
    for i, y in enumerate((ya_ref[...], yb, yc_ref[...], yd_ref[...])):
        gated = (y.astype(F32) * _silu(z_ref[:, i * GROUP_W:(i + 1) * GROUP_W].astype(F32))).astype(BF16)
        part = jnp.dot(gated, w_ref[i * GROUP_W:(i + 1) * GROUP_W, :], preferred_element_type=F32)
        acc = part if acc is None else acc + part
    ms = jnp.mean(acc * acc, axis=-1, keepdims=True)
    o_ref[...] = x_ref[...] + acc * lax.rsqrt(ms + EPS) * pg_ref[...]


def _outproj(ya, o_f, o_b, yc, yd, z, x, gdn_g, post_g, w_out, tm):
    b, l, d = x.shape
    grp = pl.BlockSpec((None, tm, GROUP_W), lambda bi, i: (bi, i, 0))
    full = pl.BlockSpec((None, tm, d), lambda bi, i: (bi, i, 0))
    const = lambda shape: pl.BlockSpec(shape, lambda bi, i: (0, 0))
    return pl.pallas_call(
        _outproj_kernel,
        grid=(b, l // tm),
        in_specs=[grp, grp, grp, grp, grp, full, full,
                  const((1, GROUP_W)), const((1, d)), const((GROUP_W, GROUP_W)), const((d, d))],
        out_specs=full,
        out_shape=jax.ShapeDtypeStruct((b, l, d), F32),
        compiler_params=_cparams(("parallel", "parallel")),
        name="outproj",
    )(ya, o_f, o_b, yc, yd, z, x, jnp.tile(gdn_g, N_HEADS).reshape(1, GROUP_W), post_g.reshape(1, d),
      _head_sum_matrix(), w_out.astype(BF16))


def _tile(l, want):
    t = min(want, l)
    assert l % t == 0
    return t


def _layer(x, mem, tables, pre_g, post_g, w_perm, w_fnet, conv_w, a_log, dt_bias, gdn_g, na_bias, mem_g, w_mem_kv,
           w_out):
    l = x.shape[1]
    u, bqkv, bgate, cqkv, dq, z = _inproj(x, pre_g, w_perm, _tile(l, 512))
    ya = _fnet(u, w_fnet, tables)
    qkvn, gb = _gdn_pre(bqkv, bgate, conv_w, a_log, dt_bias, _tile(l, 512))
    o_f, o_b = _gdn_scan(qkvn, gb, GDN_CHUNKS_PER_STEP)
    yc = _natten(cqkv, na_bias)
    yd = _memattn(dq, mem, mem_g, w_mem_kv, _tile(l, 512))
    return _outproj(ya, o_f, o_b, yc, yd, z, x, gdn_g, post_g, w_out, _tile(l, 512))


def kernel(x_prompt, x_sample, mem_prompt, mem_sample, pre_norm_g, post_norm_g, w_in, w_fnet, gdn_conv_w,
           gdn_a_log, gdn_dt_bias, gdn_norm_g, na_rpb, mem_norm_g, w_mem_kv, w_out):
    l = x_prompt.shape[1]
    assert x_prompt.shape[1:] == x_sample.shape[1:] == (l, D_MODEL)
    assert l % (NA_KH * GRID_W) == 0 and l % (GDN_CHUNKS_PER_STEP * GDN_CHUNK) == 0 and l % FFT_NA == 0
    depth = pre_norm_g.shape[0]
    tables = _fnet_tables(l)
    w_perm = [_permute_w_in(w_in[i]) for i in range(depth)]
    na_bias = [_natten_bias(na_rpb[i]) for i in range(depth)]

    def trunk(x, mem):
        for i in range(depth):
            x = _layer(x, mem, tables, pre_norm_g[i], post_norm_g[i], w_perm[i], w_fnet[i], gdn_conv_w[i],
                       gdn_a_log[i], gdn_dt_bias[i], gdn_norm_g[i], na_bias[i], mem_norm_g[i], w_mem_kv[i], w_out[i])
        return x

    return (trunk(x_prompt, mem_prompt), trunk(x_sample, mem_sample))
```

```python
import functools
import math

import numpy as np
import jax
import jax.numpy as jnp
from jax import lax
from jax.experimental import pallas as pl
from jax.experimental.pallas import tpu as pltpu

F32 = jnp.float32
BF16 = jnp.bfloat16

D_MODEL = 1024
GROUP_W = 256
HEAD_DIM = 64
HEAD_SHIFT = 6
N_HEADS = 4
FNET_BLOCK_W = 64
GDN_CHUNK = 64
CONV_K = 5
GRID_W = 64
NA_KH = 8
NA_KW = 16
N_GATE_COLS = 16
EPS = 1e-6
NEG_BIG = -1e30

LANES = 128
SUBLANES = 8
GATE_PAD = LANES
D_PROJ = GROUP_W + 3 * GROUP_W + GATE_PAD + 3 * GROUP_W + GROUP_W + 4 * GROUP_W
FFT_NA = 64
FNET_KA_PER_STEP = 8
GDN_CHUNKS_PER_STEP = 4
NA_ROWS_PER_ITER = 4
MEM_Q_SPLIT = 2
VMEM_LIMIT = 56 * 1024 * 1024


def _cparams(sem):
    return pltpu.CompilerParams(dimension_semantics=sem, vmem_limit_bytes=VMEM_LIMIT)


def _dot(a, b):
    return jnp.dot(a.astype(BF16), b.astype(BF16), preferred_element_type=F32)


def _dot_nt(a, b):
    return lax.dot_general(a.astype(BF16), b.astype(BF16), (((1,), (1,)), ((), ())),
                           preferred_element_type=F32)


def _dot_split(a, b_bf16):
    hi = a.astype(BF16)
    lo = (a - hi.astype(F32)).astype(BF16)
    return (jnp.dot(hi, b_bf16, preferred_element_type=F32)
            + jnp.dot(lo, b_bf16, preferred_element_type=F32))


def _sigmoid(x):
    return 1.0 / (1.0 + jnp.exp(-x))


def _silu(x):
    return x * _sigmoid(x)


def _head_sum_matrix():
    idx = np.arange(GROUP_W) // HEAD_DIM
    return jnp.asarray((idx[:, None] == idx[None, :]).astype(np.float32), dtype=BF16)


def _inproj_kernel(x_ref, g_ref, w_ref, u_ref, bqkv_ref, bg_ref, cqkv_ref, dq_ref, z_ref):
    x = x_ref[...]
    ms = jnp.mean(x * x, axis=-1, keepdims=True)
    h = (x * lax.rsqrt(ms + EPS) * g_ref[...]).astype(BF16)
    off = 0
    for ref in (u_ref, bqkv_ref, bg_ref, cqkv_ref, dq_ref, z_ref):
        n = ref.shape[-1]
        ref[...] = jnp.dot(h, w_ref[:, off:off + n], preferred_element_type=F32).astype(ref.dtype)
        off += n


def _permute_w_in(w_in):
    g = GROUP_W
    o_az, o_bqkv, o_bz, o_bg = g, 2 * g, 5 * g, 6 * g
    o_cqkv = o_bg + N_GATE_COLS
    o_cz, o_dq, o_dz = o_cqkv + 3 * g, o_cqkv + 4 * g, o_cqkv + 5 * g
    pad = jnp.zeros((D_MODEL, GATE_PAD - N_GATE_COLS), w_in.dtype)
    cols = [w_in[:, 0:g], w_in[:, o_bqkv:o_bz], w_in[:, o_bg:o_cqkv], pad, w_in[:, o_cqkv:o_cz],
            w_in[:, o_dq:o_dz], w_in[:, o_az:o_bqkv], w_in[:, o_bz:o_bg], w_in[:, o_cz:o_dq],
            w_in[:, o_dz:o_dz + g]]
    return jnp.concatenate(cols, axis=1).astype(BF16)


def _inproj(x, pre_g, w_perm, tm):
    b, l, d = x.shape
    widths = (GROUP_W, 3 * GROUP_W, GATE_PAD, 3 * GROUP_W, GROUP_W, 4 * GROUP_W)
    dtypes = (BF16, BF16, F32, BF16, BF16, BF16)
    row = lambda n: pl.BlockSpec((None, tm, n), lambda bi, i: (bi, i, 0))
    return pl.pallas_call(
        _inproj_kernel,
        grid=(b, l // tm),
        in_specs=[row(d),
                  pl.BlockSpec((1, d), lambda bi, i: (0, 0)),
                  pl.BlockSpec((d, D_PROJ), lambda bi, i: (0, 0))],
        out_specs=[row(n) for n in widths],
        out_shape=[jax.ShapeDtypeStruct((b, l, n), dt) for n, dt in zip(widths, dtypes)],
        compiler_params=_cparams(("parallel", "parallel")),
        name="inproj",
    )(x, pre_g.reshape(1, d), w_perm)


def _fnet_a_kernel(x_ref, t_ref, y_ref):
    y_ref[...] = jnp.dot(t_ref[...], x_ref[...].astype(BF16), preferred_element_type=F32).astype(BF16)


def _fnet_b_kernel(y_ref, g_ref, bcs_ref, wf_ref, o_ref):
    nk, nb = o_ref.shape[0], o_ref.shape[1]
    bc = bcs_ref[:GROUP_W, :]
    bs = bcs_ref[GROUP_W:, :]
    wf = wf_ref[...]
    z = [jnp.dot(g_ref[i], y_ref[i], preferred_element_type=F32) for i in range(nk)]
    f = [jnp.dot(a[:nb].astype(BF16), bc, preferred_element_type=F32)
         + jnp.dot(a[nb:].astype(BF16), bs, preferred_element_type=F32) for a in z]
    for i, a in enumerate(f):
        o_ref[i] = jnp.dot(a.astype(BF16), wf, preferred_element_type=F32).astype(o_ref.dtype)


def _fnet_tables(l):
    na, nb = FFT_NA, l // FFT_NA
    ia = np.arange(na)
    ang_a = 2.0 * np.pi * ((ia[:, None] * ia[None, :]) % na) / na
    ta = np.concatenate([np.cos(ang_a), -np.sin(ang_a)], axis=0) / math.sqrt(na)
    ka = jnp.arange(na, dtype=jnp.int32)[:, None, None]
    kb = jnp.arange(nb, dtype=jnp.int32)[None, :, None]
    n2 = jnp.arange(nb, dtype=jnp.int32)[None, None, :]
    ang = (2.0 * math.pi / l) * ((n2 * (ka + na * kb)) % l).astype(F32)
    cg = jnp.cos(ang) / math.sqrt(nb)
    sg = jnp.sin(ang) / math.sqrt(nb)
    gbig = jnp.concatenate([jnp.concatenate([cg, sg], axis=2),
                            jnp.concatenate([-sg, cg], axis=2)], axis=1).astype(BF16)
    ic = np.arange(GROUP_W)
    same = (ic[:, None] // FNET_BLOCK_W) == (ic[None, :] // FNET_BLOCK_W)
    ang_c = 2.0 * np.pi * (((ic[:, None] % FNET_BLOCK_W) * (ic[None, :] % FNET_BLOCK_W)) % FNET_BLOCK_W) / FNET_BLOCK_W
    bc = np.where(same, np.cos(ang_c), 0.0) / math.sqrt(FNET_BLOCK_W)
    bs = np.where(same, np.sin(ang_c), 0.0) / math.sqrt(FNET_BLOCK_W)
    bcs = np.concatenate([bc, bs], axis=0)
    return jnp.asarray(ta, dtype=F32).astype(BF16), gbig, jnp.asarray(bcs, dtype=F32).astype(BF16)


def _fnet(u, w_fnet, tables):
    b, l, c = u.shape
    na, nb = FFT_NA, l // FFT_NA
    ta, gbig, bcs = tables
    cw = min(2048, nb * c)
    x2 = u.reshape(b, na, nb * c)
    y = pl.pallas_call(
        _fnet_a_kernel,
        grid=(b, nb * c // cw),
        in_specs=[pl.BlockSpec((None, na, cw), lambda bi, j: (bi, 0, j)),
                  pl.BlockSpec((2 * na, na), lambda bi, j: (0, 0))],
        out_specs=pl.BlockSpec((None, 2 * na, cw), lambda bi, j: (bi, 0, j)),
        out_shape=jax.ShapeDtypeStruct((b, 2 * na, nb * c), BF16),
        compiler_params=_cparams(("parallel", "parallel")),
        name="fnet_a",
    )(x2, ta)
    ys = y.reshape(b, 2, na, nb, c).transpose(0, 2, 1, 3, 4).reshape(b, na, 2 * nb, c)
    o = pl.pallas_call(
        _fnet_b_kernel,
        grid=(b, na // FNET_KA_PER_STEP),
        in_specs=[pl.BlockSpec((None, FNET_KA_PER_STEP, 2 * nb, c), lambda bi, k: (bi, k, 0, 0)),
                  pl.BlockSpec((FNET_KA_PER_STEP, 2 * nb, 2 * nb), lambda bi, k: (k, 0, 0)),
                  pl.BlockSpec((2 * c, c), lambda bi, k: (0, 0)),
                  pl.BlockSpec((c, c), lambda bi, k: (0, 0))],
        out_specs=pl.BlockSpec((None, FNET_KA_PER_STEP, nb, c), lambda bi, k: (bi, k, 0, 0)),
        out_shape=jax.ShapeDtypeStruct((b, na, nb, c), BF16),
        compiler_params=_cparams(("parallel", "parallel")),
        name="fnet_b",
    )(ys, gbig, bcs, w_fnet.astype(BF16))
    return o.transpose(0, 2, 1, 3).reshape(b, l, c)


def _gdn_pre_kernel(cur_ref, prev_ref, next_ref, gate_ref, cw_ref, gp_ref, hs_ref, tri_ref, qkv_ref, gb_ref, win_ref):
    i = pl.program_id(1)
    n = pl.num_programs(1)
    tg = cur_ref.shape[0]
    halo = prev_ref.shape[0]
    win_ref[0:halo, :] = prev_ref[...].astype(F32) * (i > 0).astype(F32)
    win_ref[halo:halo + tg, :] = cur_ref[...].astype(F32)
    win_ref[halo + tg:, :] = next_ref[...].astype(F32) * (i < n - 1).astype(F32)
    pad = CONV_K // 2
    acc = cw_ref[0:1, :] * win_ref[pl.ds(halo - pad, tg), :]
    for t in range(1, CONV_K):
        acc = acc + cw_ref[t:t + 1, :] * win_ref[pl.ds(halo - pad + t, tg), :]
    act = _silu(acc)
    hs = hs_ref[...]
    q = act[:, 0:GROUP_W]
    k = act[:, GROUP_W:2 * GROUP_W]
    qn = q * lax.rsqrt(_dot_split(q * q, hs) + EPS) * (HEAD_DIM ** -0.5)
    kn = k * lax.rsqrt(_dot_split(k * k, hs) + EPS)
    qkv_ref[:, 0:GROUP_W] = qn.astype(qkv_ref.dtype)
    qkv_ref[:, GROUP_W:2 * GROUP_W] = kn.astype(qkv_ref.dtype)
    qkv_ref[:, 2 * GROUP_W:] = act[:, 2 * GROUP_W:].astype(qkv_ref.dtype)
    a = gate_ref[...]
    x = a + gp_ref[1:2, :]
    softplus = jnp.maximum(x, 0.0) + jnp.log(1.0 + jnp.exp(-jnp.abs(x)))
    g = -jnp.exp(gp_ref[0:1, :]) * softplus
    gcf = _dot_split_left(tri_ref[0], g)
    gcb = _dot_split_left(tri_ref[1], g)
    lane = lax.broadcasted_iota(jnp.int32, a.shape, 1)
    gb_ref[...] = jnp.where(lane < N_HEADS, gcf, jnp.where(lane < 2 * N_HEADS, gcb, _sigmoid(a)))


def _gdn_pre(bqkv, gates, conv_w, a_log, dt_bias, tg):
    b, l, c3 = bqkv.shape
    halo = 2 * SUBLANES
    nt = l // tg
    per = tg // halo
    nh = l // halo
    cw = jnp.zeros((SUBLANES, c3), F32).at[:CONV_K].set(conv_w)
    gp = jnp.zeros((SUBLANES, GATE_PAD), F32)
    gp = gp.at[0, :N_GATE_COLS // 2].set(a_log.reshape(-1)).at[1, :N_GATE_COLS // 2].set(dt_bias.reshape(-1))
    ti = np.arange(tg)
    same_chunk = (ti[:, None] // GDN_CHUNK) == (ti[None, :] // GDN_CHUNK)
    tri = np.stack([same_chunk & (ti[None, :] <= ti[:, None]), same_chunk & (ti[None, :] >= ti[:, None])])
    tri = jnp.asarray(tri.astype(np.float32), dtype=BF16)
    return pl.pallas_call(
        _gdn_pre_kernel,
        grid=(b, nt),
        in_specs=[pl.BlockSpec((None, tg, c3), lambda bi, i: (bi, i, 0)),
                  pl.BlockSpec((None, halo, c3), lambda bi, i: (bi, jnp.maximum(i * per - 1, 0), 0)),
                  pl.BlockSpec((None, halo, c3), lambda bi, i: (bi, jnp.minimum((i + 1) * per, nh - 1), 0)),
                  pl.BlockSpec((None, tg, GATE_PAD), lambda bi, i: (bi, i, 0)),
                  pl.BlockSpec((SUBLANES, c3), lambda bi, i: (0, 0)),
                  pl.BlockSpec((SUBLANES, GATE_PAD), lambda bi, i: (0, 0)),
                  pl.BlockSpec((GROUP_W, GROUP_W), lambda bi, i: (0, 0)),
                  pl.BlockSpec((2, tg, tg), lambda bi, i: (0, 0, 0))],
        out_specs=[pl.BlockSpec((None, tg, c3), lambda bi, i: (bi, i, 0)),
                   pl.BlockSpec((None, tg, GATE_PAD), lambda bi, i: (bi, i, 0))],
        out_shape=[jax.ShapeDtypeStruct((b, l, c3), BF16),
                   jax.ShapeDtypeStruct((b, l, GATE_PAD), F32)],
        scratch_shapes=[pltpu.VMEM((tg + 2 * halo, c3), F32)],
        compiler_params=_cparams(("parallel", "parallel")),
        name="gdn_pre",
    )(bqkv, bqkv, bqkv, gates, cw, gp, _head_sum_matrix(), tri)


def _dot_split_left(m_bf16, a):
    hi = a.astype(BF16)
    lo = (a - hi.astype(F32)).astype(BF16)
    return (jnp.dot(m_bf16, hi, preferred_element_type=F32)
            + jnp.dot(m_bf16, lo, preferred_element_type=F32))


def _block_diag(x, mask01):
    xb = x.astype(BF16)
    return jnp.concatenate([xb] * N_HEADS, axis=0) * mask01


def _gdn_scan_kernel(xf_ref, gbf_ref, xb_ref, gbb_ref, ex_ref, bdm_ref, of_ref, ob_ref,
                     s_ref, u_ref, wq_ref, qk_ref, kdt_ref, el_ref):
    c = GDN_CHUNK
    ng = xf_ref.shape[0] // c
    step = pl.program_id(1)
    slot_w = step % 2
    slot_r = 1 - slot_w

    @pl.when(step == 0)
    def _():
        for ref in (s_ref, u_ref, wq_ref, qk_ref, kdt_ref, el_ref):
            ref[...] = jnp.zeros_like(ref)

    row = lax.broadcasted_iota(jnp.int32, (c, GROUP_W), 0)
    lane = lax.broadcasted_iota(jnp.int32, (c, GROUP_W), 1)
    colm = lane & (HEAD_DIM - 1)
    lane_head = lane >> HEAD_SHIFT
    incl = [(row >= colm), (row <= colm)]
    strict = [(row > colm), (row < colm)]
    eye_f = (row == colm).astype(F32)
    bd = lambda a: _block_diag(a, bdm_ref[0])

    streams = [(g, d) for g in range(ng) for d in range(2)]

    def rows_of(g, d):
        lc = g if d == 0 else ng - 1 - g
        return pl.ds(lc * c, c)

    x_refs = (xf_ref, xb_ref)
    gb_refs = (gbf_ref, gbb_ref)
    o_refs = (of_ref, ob_ref)

    def pre_phase():
        gb = [gb_refs[d][rows_of(g, d), :] for g, d in streams]
        gb_hi = [a.astype(BF16) for a in gb]
        gb_hl = [jnp.concatenate([hi, (a - hi.astype(F32)).astype(BF16)], axis=1) for a, hi in zip(gb, gb_hi)]
        ex = [jnp.dot(a, ex_ref[d], preferred_element_type=F32) for (g, d), a in zip(streams, gb_hl)]
        gc = [a[:, :GROUP_W] for a in ex]
        beta = [a[:, GROUP_W:] for a in ex]
        yield
        gbt = [a.T for a in gb]
        gcr = [jnp.concatenate([t[N_HEADS * d + h:N_HEADS * d + h + 1, :] for h in range(N_HEADS)], axis=1)
               for (g, d), t in zip(streams, gbt)]
        decay = [jnp.exp(jnp.where(incl[d], a - r, -jnp.inf)) for (g, d), a, r in zip(streams, gc, gcr)]
        last = [c - 1, 0]
        g_last = [a[last[d]:last[d] + 1, :] for (g, d), a in zip(streams, gc)]
        e_g = [jnp.exp(a) for a in gc]
        e_rem = [jnp.exp(gl - a) for gl, a in zip(g_last, gc)]
        for i, gl in enumerate(g_last):
            el_ref[slot_w, i] = jnp.broadcast_to(jnp.exp(gl), (SUBLANES, GROUP_W))
        yield
        q = [x_refs[d][rows_of(g, d), 0:GROUP_W] for g, d in streams]
        k = [x_refs[d][rows_of(g, d), GROUP_W:2 * GROUP_W] for g, d in streams]
        v = [x_refs[d][rows_of(g, d), 2 * GROUP_W:3 * GROUP_W] for g, d in streams]
        kb = [a * b for a, b in zip(k, beta)]
        kq = [_dot_nt(jnp.concatenate([a, b], axis=0), bd(kk)) for a, b, kk in zip(kb, q, k)]
        lm = [jnp.where(strict[d], a[:c] * dec, 0.0) for (g, d), a, dec in zip(streams, kq, decay)]
        yield
        for i, (a, dec) in enumerate(zip(kq, decay)):
            qk_ref[slot_w, i] = (a[c:] * dec).astype(BF16)
        for i, (a, e) in enumerate(zip(k, e_rem)):
            kdt_ref[slot_w, i] = (a * e).T.astype(BF16)
        minv = [eye_f - jnp.where((row >> 1) == (colm >> 1), a, 0.0) for a in lm]
        lm_b = [a.astype(BF16) for a in lm]
        sh = 1
        while (2 << sh) <= c:
            t = [_dot(m, _block_diag(a, bdm_ref[sh])) for m, a in zip(minv, lm_b)]
            yield
            t = [_dot(a, bd(m)) for a, m in zip(t, minv)]
            minv = [m - a for m, a in zip(minv, t)]
            yield
            sh += 1
        for i, (m, a, b) in enumerate(zip(minv, v, beta)):
            u_ref[slot_w, i] = _dot(m, bd(a * b))
        yield
        for i, (m, a, e, qq) in enumerate(zip(minv, kb, e_g, q)):
            wq_ref[slot_w, i] = jnp.concatenate([_dot(m, bd(a * e)), qq * e], axis=0).astype(BF16)

    def state_phase():
        for g in range(ng):
            idx = [2 * g, 2 * g + 1]
            state = [s_ref[d] for d in range(2)]
            ws = [jnp.dot(wq_ref[slot_r, i], bd(s), preferred_element_type=F32) for i, s in zip(idx, state)]
            yield
            v_new = [u_ref[slot_r, i] - a[:c] for i, a in zip(idx, ws)]
            o2 = [jnp.dot(qk_ref[slot_r, i], bd(a), preferred_element_type=F32) for i, a in zip(idx, v_new)]
            upd = [jnp.dot(kdt_ref[slot_r, i], a.astype(BF16), preferred_element_type=F32)
                   for i, a in zip(idx, v_new)]
            for d in range(2):
                o_refs[d][rows_of(g, d), :] = (ws[d][c:] + o2[d]).astype(o_refs[d].dtype)
                fold = jnp.where(lane_head == 0, upd[d][0:c], 0.0)
                for h in range(1, N_HEADS):
                    fold = fold + jnp.where(lane_head == h, upd[d][h * c:(h + 1) * c], 0.0)
                s_ref[d] = state[d] * el_ref[slot_r, idx[d], 0:1, :] + fold
            yield

    phases = [pre_phase(), state_phase()]
    while phases:
        for p in list(phases):
            if next(p, StopIteration) is StopIteration:
                phases.remove(p)


def _gate_expanders():
    e = np.zeros((2, 2 * GATE_PAD, 2 * GROUP_W), np.float32)
    for d in range(2):
        for h in range(N_HEADS):
            for half in range(2):
                r0 = half * GATE_PAD
                e[d, r0 + N_HEADS * d + h, h * HEAD_DIM:(h + 1) * HEAD_DIM] = 1.0
                e[d, r0 + N_GATE_COLS // 2 + N_HEADS * d + h, GROUP_W + h * HEAD_DIM:GROUP_W + (h + 1) * HEAD_DIM] = 1.0
    return jnp.asarray(e, dtype=BF16)


def _block_diag_masks():
    i = np.arange(GROUP_W)
    r, cc = i[:, None], i[None, :]
    same_head = (r >> HEAD_SHIFT) == (cc >> HEAD_SHIFT)
    rl, cl = r & (HEAD_DIM - 1), cc & (HEAD_DIM - 1)
    masks = [same_head]
    sh = 1
    while (2 << sh) <= HEAD_DIM:
        masks.append(same_head & ((rl >> (sh + 1)) == (cl >> (sh + 1))) & ((rl >> sh) != (cl >> sh)))
        sh += 1
    return jnp.asarray(np.stack(masks).astype(np.float32), dtype=BF16)


def _gdn_scan(qkvn, gb, ng):
    b, l, c3 = qkvn.shape
    rows = ng * GDN_CHUNK
    nsteps = l // rows
    grp_in = lambda i: jnp.minimum(i, nsteps - 1)
    grp_out = lambda i: jnp.maximum(i - 1, 0)
    fwd = lambda w, f: pl.BlockSpec((None, rows, w), lambda bi, i: (bi, f(i), 0))
    bwd = lambda w, f: pl.BlockSpec((None, rows, w), lambda bi, i: (bi, nsteps - 1 - f(i), 0))
    ns = 2 * ng
    c = GDN_CHUNK
    masks = _block_diag_masks()
    return pl.pallas_call(
        _gdn_scan_kernel,
        grid=(b, nsteps + 1),
        in_specs=[fwd(c3, grp_in), fwd(GATE_PAD, grp_in), bwd(c3, grp_in), bwd(GATE_PAD, grp_in),
                  pl.BlockSpec((2, 2 * GATE_PAD, 2 * GROUP_W), lambda bi, i: (0, 0, 0)),
                  pl.BlockSpec(masks.shape, lambda bi, i: (0, 0, 0))],
        out_specs=[fwd(GROUP_W, grp_out), bwd(GROUP_W, grp_out)],
        out_shape=[jax.ShapeDtypeStruct((b, l, GROUP_W), BF16)] * 2,
        scratch_shapes=[pltpu.VMEM((2, HEAD_DIM, GROUP_W), F32),
                        pltpu.VMEM((2, ns, c, GROUP_W), F32),
                        pltpu.VMEM((2, ns, 2 * c, GROUP_W), BF16),
                        pltpu.VMEM((2, ns, c, GROUP_W), BF16),
                        pltpu.VMEM((2, ns, GROUP_W, c), BF16),
                        pltpu.VMEM((2, ns, SUBLANES, GROUP_W), F32)],
        compiler_params=_cparams(("parallel", "arbitrary")),
        name="gdn_scan",
    )(qkvn, gb, qkvn, gb, _gate_expanders(), masks)


def _head_stack(q):
    lane_head = lax.broadcasted_iota(jnp.int32, q.shape, 1) >> HEAD_SHIFT
    zero = jnp.zeros((), q.dtype)
    return jnp.concatenate([jnp.where(lane_head == h, q, zero) for h in range(N_HEADS)], axis=0)


def _head_unstack(pv, m):
    lane_head = lax.broadcasted_iota(jnp.int32, (m, GROUP_W), 1) >> HEAD_SHIFT
    out = jnp.where(lane_head == 0, pv[0:m], 0.0)
    for h in range(1, N_HEADS):
        out = out + jnp.where(lane_head == h, pv[h * m:(h + 1) * m], 0.0)
    return out


def _attend(qs, ks, vs, biases):
    s = [_dot_nt(a, b) for a, b in zip(qs, ks)]
    s = [a if b is None else a + b for a, b in zip(s, biases)]
    m = [jnp.max(a, axis=-1, keepdims=True) for a in s]
    p = [jnp.exp(a - b) for a, b in zip(s, m)]
    den = [jnp.sum(a, axis=-1, keepdims=True) for a in p]
    pv = [jnp.dot(a.astype(BF16), b, preferred_element_type=F32) for a, b in zip(p, vs)]
    return [a / b for a, b in zip(pv, den)]


def _natten_kernel(q_ref, kp_ref, kc_ref, kn_ref, vp_ref, vc_ref, vn_ref, bias_ref, o_ref, kwin_ref, vwin_ref):
    gi = pl.program_id(1)
    rows = pl.num_programs(1) * NA_KH
    blk = kc_ref.shape[0]
    for t, (kr, vr) in enumerate(((kp_ref, vp_ref), (kc_ref, vc_ref), (kn_ref, vn_ref))):
        kwin_ref[t * blk:(t + 1) * blk, :] = kr[...]
        vwin_ref[t * blk:(t + 1) * blk, :] = vr[...]

    def body(t, carry):
        js = [t * NA_ROWS_PER_ITER + i for i in range(NA_ROWS_PER_ITER)]
        r = [gi * NA_KH + j for j in js]
        rs = [jnp.clip(a - NA_KH // 2, 0, rows - NA_KH) for a in r]
        start = [pl.multiple_of((a - (gi - 1) * NA_KH) * GRID_W, GRID_W) for a in rs]
        qrows = [pl.ds(pl.multiple_of(j * GRID_W, GRID_W), GRID_W) for j in js]
        qs = [_head_stack(q_ref[a, :] * (HEAD_DIM ** -0.5)) for a in qrows]
        kw = [kwin_ref[pl.ds(a, NA_KH * GRID_W), :] for a in start]
        vw = [vwin_ref[pl.ds(a, NA_KH * GRID_W), :] for a in start]
        bias = [bias_ref[a - b] for a, b in zip(r, rs)]
        for a, pv in zip(qrows, _attend(qs, kw, vw, bias)):
            o_ref[a, :] = _head_unstack(pv, GRID_W).astype(o_ref.dtype)
        return carry

    lax.fori_loop(0, NA_KH // NA_ROWS_PER_ITER, body, 0)


def _natten_bias(rpb):
    w = np.arange(GRID_W)
    cs = np.clip(w - NA_KW // 2, 0, GRID_W - NA_KW)
    wk = np.arange(GRID_W)
    in_win = (wk[None, :] >= cs[:, None]) & (wk[None, :] < cs[:, None] + NA_KW)
    col_off = wk[None, :] - w[:, None] + NA_KW - 1
    delta = np.arange(NA_KH)
    i = np.arange(NA_KH)
    row_off = i[None, :] - delta[:, None] + NA_KH - 1
    sel_r = (row_off[:, :, None] == np.arange(2 * NA_KH - 1)).astype(np.float32)
    sel_c = ((col_off[:, :, None] == np.arange(2 * NA_KW - 1)) & in_win[:, :, None]).astype(np.float32)
    t = jnp.einsum('hab,dia,wvb->hdiwv', rpb.astype(F32), sel_r, sel_c, precision=lax.Precision.HIGHEST)
    t = jnp.where(jnp.asarray(in_win)[None, None, None], t, NEG_BIG)
    t = t.transpose(1, 0, 3, 2, 4)
    return t.reshape(NA_KH, N_HEADS * GRID_W, NA_KH * GRID_W).astype(F32)


def _natten(cqkv, bias):
    b, l, _ = cqkv.shape
    blk = NA_KH * GRID_W
    ng = l // blk
    spec = lambda colblk, f: pl.BlockSpec((None, blk, GROUP_W), lambda bi, g: (bi, f(g), colblk))
    cur = lambda g: g
    prev = lambda g: jnp.maximum(g - 1, 0)
    nxt = lambda g: jnp.minimum(g + 1, ng - 1)
    return pl.pallas_call(
        _natten_kernel,
        grid=(b, ng),
        in_specs=[spec(0, cur), spec(1, prev), spec(1, cur), spec(1, nxt),
                  spec(2, prev), spec(2, cur), spec(2, nxt),
                  pl.BlockSpec(bias.shape, lambda bi, g: (0, 0, 0))],
        out_specs=pl.BlockSpec((None, blk, GROUP_W), lambda bi, g: (bi, g, 0)),
        out_shape=jax.ShapeDtypeStruct((b, l, GROUP_W), BF16),
        scratch_shapes=[pltpu.VMEM((3 * blk, GROUP_W), BF16), pltpu.VMEM((3 * blk, GROUP_W), BF16)],
        compiler_params=_cparams(("parallel", "parallel")),
        name="natten",
    )(cqkv, cqkv, cqkv, cqkv, cqkv, cqkv, cqkv, bias)


def _memattn(q_ref, k_ref, v_ref):
    ts = q_ref.shape[0] // MEM_Q_SPLIT
    parts = [pl.ds(i * ts, ts) for i in range(MEM_Q_SPLIT)]
    qs = [_head_stack(q_ref[a, :] * (HEAD_DIM ** -0.5)) for a in parts]
    k = k_ref[...]
    v = v_ref[...]
    pv = _attend(qs, [k] * MEM_Q_SPLIT, [v] * MEM_Q_SPLIT, [None] * MEM_Q_SPLIT)
    return [_head_unstack(a, ts) for a in pv]


def _outproj_kernel(ya_ref, of_ref, ob_ref, yc_ref, dq_ref, z_ref, x_ref, mem_ref, mg_ref, wkv_ref,
                    gg_ref, pg_ref, hs_ref, w_ref, o_ref, k_ref, v_ref):
    @pl.when(pl.program_id(1) == 0)
    def _():
        m = mem_ref[...]
        ms = jnp.mean(m * m, axis=-1, keepdims=True)
        mn = (m * lax.rsqrt(ms + EPS) * mg_ref[...]).astype(BF16)
        kv = jnp.dot(mn, wkv_ref[...], preferred_element_type=F32)
        k_ref[...] = kv[:, :GROUP_W].astype(BF16)
        v_ref[...] = kv[:, GROUP_W:].astype(BF16)

    yd = jnp.concatenate(_memattn(dq_ref, k_ref, v_ref), axis=0)
    o = of_ref[...].astype(F32) + ob_ref[...].astype(F32)
    ms = _dot_split(o * o, hs_ref[...]) * (1.0 / HEAD_DIM)
    yb = o * lax.rsqrt(ms + EPS) * gg_ref[...]
    acc = None
    for i, y in enumerate((ya_ref[...], yb, yc_ref[...], yd)):
        gated = (y.astype(F32) * _silu(z_ref[:, i * GROUP_W:(i + 1) * GROUP_W].astype(F32))).astype(BF16)
        part = jnp.dot(gated, w_ref[i * GROUP_W:(i + 1) * GROUP_W, :], preferred_element_type=F32)
        acc = part if acc is None else acc + part
    ms = jnp.mean(acc * acc, axis=-1, keepdims=True)
    o_ref[...] = x_ref[...] + acc * lax.rsqrt(ms + EPS) * pg_ref[...]


def _outproj(ya, o_f, o_b, yc, dq, z, x, mem, mem_g, w_mem_kv, gdn_g, post_g, w_out, tm):
    b, l, d = x.shape
    m = mem.shape[1]
    grp = pl.BlockSpec((None, tm, GROUP_W), lambda bi, i: (bi, i, 0))
    full = pl.BlockSpec((None, tm, d), lambda bi, i: (bi, i, 0))
    const = lambda shape: pl.BlockSpec(shape, lambda bi, i: (0, 0))
    return pl.pallas_call(
        _outproj_kernel,
        grid=(b, l // tm),
        in_specs=[grp, grp, grp, grp, grp, full, full,
                  pl.BlockSpec((None, m, d), lambda bi, i: (bi, 0, 0)), const((1, d)), const((d, 2 * GROUP_W)),
                  const((1, GROUP_W)), const((1, d)), const((GROUP_W, GROUP_W)), const((d, d))],
        out_specs=full,
        out_shape=jax.ShapeDtypeStruct((b, l, d), F32),
        scratch_shapes=[pltpu.VMEM((m, GROUP_W), BF16), pltpu.VMEM((m, GROUP_W), BF16)],
        compiler_params=_cparams(("parallel", "arbitrary")),
        name="outproj",
    )(ya, o_f, o_b, yc, dq, z, x, mem, mem_g.reshape(1, d), w_mem_kv.astype(BF16),
      jnp.tile(gdn_g, N_HEADS).reshape(1, GROUP_W), post_g.reshape(1, d), _head_sum_matrix(), w_out.astype(BF16))


def _tile(l, want):
    t = min(want, l)
    assert l % t == 0
    return t


def _layer(x, mem, tables, pre_g, post_g, w_perm, w_fnet, conv_w, a_log, dt_bias, gdn_g, na_bias, mem_g, w_mem_kv,
           w_out):
    l = x.shape[1]
    u, bqkv, bgate, cqkv, dq, z = _inproj(x, pre_g, w_perm, _tile(l, 1024))
    ya = _fnet(u, w_fnet, tables)
    qkvn, gb = _gdn_pre(bqkv, bgate, conv_w, a_log, dt_bias, _tile(l, 512))
    o_f, o_b = _gdn_scan(qkvn, gb, GDN_CHUNKS_PER_STEP)
    yc = _natten(cqkv, na_bias)
    return _outproj(ya, o_f, o_b, yc, dq, z, x, mem, mem_g, w_mem_kv, gdn_g, post_g, w_out, _tile(l, 512))


def kernel(x_prompt, x_sample, mem_prompt, mem_sample, pre_norm_g, post_norm_g, w_in, w_fnet, gdn_conv_w,
           gdn_a_log, gdn_dt_bias, gdn_norm_g, na_rpb, mem_norm_g, w_mem_kv, w_out):
    l = x_prompt.shape[1]
    assert x_prompt.shape[1:] == x_sample.shape[1:] == (l, D_MODEL)
    assert l % (NA_KH * GRID_W) == 0 and l % (GDN_CHUNKS_PER_STEP * GDN_CHUNK) == 0 and l % FFT_NA == 0
    depth = pre_norm_g.shape[0]
    tables = _fnet_tables(l)
    w_perm = [_permute_w_in(w_in[i]) for i in range(depth)]
    na_bias = [_natten_bias(na_rpb[i]) for i in range(depth)]

    def trunk(x, mem):
        for i in range(depth):
            x = _layer(x, mem, tables, pre_norm_g[i], post_norm_g[i], w_perm[i], w_fnet[i], gdn_conv_w[i],
                       gdn_a_log[i], gdn_dt_bias[i], gdn_norm_g[i], na_bias[i], mem_norm_g[i], w_mem_kv[i], w_out[i])
        return x

    return (trunk(x_prompt, mem_prompt), trunk(x_sample, mem_sample))
```

```python
import functools
import math

import numpy as np
import jax
import jax.numpy as jnp
from jax import lax
from jax.experimental import pallas as pl
from jax.experimental.pallas import tpu as pltpu

F32 = jnp.float32
BF16 = jnp.bfloat16

D_MODEL = 1024
GROUP_W = 256
HEAD_DIM = 64
HEAD_SHIFT = 6
N_HEADS = 4
FNET_BLOCK_W = 64
GDN_CHUNK = 64
CONV_K = 5
GRID_W = 64
NA_KH = 8
NA_KW = 16
N_GATE_COLS = 16
EPS = 1e-6
NEG_BIG = -1e30

LANES = 128
SUBLANES = 8
GATE_PAD = LANES
D_PROJ = GROUP_W + 3 * GROUP_W + GATE_PAD + 3 * GROUP_W + GROUP_W + 4 * GROUP_W
FFT_NA = 64
FNET_KA_PER_STEP = 8
GDN_CHUNKS_PER_STEP = 4
NA_ROWS_PER_ITER = 4
MEM_Q_SPLIT = 4
VMEM_LIMIT = 56 * 1024 * 1024


def _cparams(sem):
    return pltpu.CompilerParams(dimension_semantics=sem, vmem_limit_bytes=VMEM_LIMIT)


def _dot(a, b):
    return jnp.dot(a.astype(BF16), b.astype(BF16), preferred_element_type=F32)


def _dot_nt(a, b):
    return lax.dot_general(a.astype(BF16), b.astype(BF16), (((1,), (1,)), ((), ())),
                           preferred_element_type=F32)


def _dot_split(a, b_bf16):
    hi = a.astype(BF16)
    lo = (a - hi.astype(F32)).astype(BF16)
    return (jnp.dot(hi, b_bf16, preferred_element_type=F32)
            + jnp.dot(lo, b_bf16, preferred_element_type=F32))


def _sigmoid(x):
    return 1.0 / (1.0 + jnp.exp(-x))


def _silu(x):
    return x * _sigmoid(x)


def _head_sum_matrix():
    idx = np.arange(GROUP_W) // HEAD_DIM
    return jnp.asarray((idx[:, None] == idx[None, :]).astype(np.float32), dtype=BF16)


def _inproj_kernel(x_ref, g_ref, w_ref, u_ref, bqkv_ref, bg_ref, cqkv_ref, dq_ref, z_ref):
    x = x_ref[...]
    ms = jnp.mean(x * x, axis=-1, keepdims=True)
    h = (x * lax.rsqrt(ms + EPS) * g_ref[...]).astype(BF16)
    off = 0
    for ref in (u_ref, bqkv_ref, bg_ref, cqkv_ref, dq_ref, z_ref):
        n = ref.shape[-1]
        ref[...] = jnp.dot(h, w_ref[:, off:off + n], preferred_element_type=F32).astype(ref.dtype)
        off += n


def _permute_w_in(w_in):
    g = GROUP_W
    o_az, o_bqkv, o_bz, o_bg = g, 2 * g, 5 * g, 6 * g
    o_cqkv = o_bg + N_GATE_COLS
    o_cz, o_dq, o_dz = o_cqkv + 3 * g, o_cqkv + 4 * g, o_cqkv + 5 * g
    pad = jnp.zeros((D_MODEL, GATE_PAD - N_GATE_COLS), w_in.dtype)
    cols = [w_in[:, 0:g], w_in[:, o_bqkv:o_bz], w_in[:, o_bg:o_cqkv], pad, w_in[:, o_cqkv:o_cz],
            w_in[:, o_dq:o_dz], w_in[:, o_az:o_bqkv], w_in[:, o_bz:o_bg], w_in[:, o_cz:o_dq],
            w_in[:, o_dz:o_dz + g]]
    return jnp.concatenate(cols, axis=1).astype(BF16)


def _inproj(x, pre_g, w_perm, tm):
    b, l, d = x.shape
    widths = (GROUP_W, 3 * GROUP_W, GATE_PAD, 3 * GROUP_W, GROUP_W, 4 * GROUP_W)
    dtypes = (BF16, BF16, F32, BF16, BF16, BF16)
    row = lambda n: pl.BlockSpec((None, tm, n), lambda bi, i: (bi, i, 0))
    return pl.pallas_call(
        _inproj_kernel,
        grid=(b, l // tm),
        in_specs=[row(d),
                  pl.BlockSpec((1, d), lambda bi, i: (0, 0)),
                  pl.BlockSpec((d, D_PROJ), lambda bi, i: (0, 0))],
        out_specs=[row(n) for n in widths],
        out_shape=[jax.ShapeDtypeStruct((b, l, n), dt) for n, dt in zip(widths, dtypes)],
        compiler_params=_cparams(("parallel", "parallel")),
        name="inproj",
    )(x, pre_g.reshape(1, d), w_perm)


def _fnet_a_kernel(x_ref, t_ref, y_ref):
    y_ref[...] = jnp.dot(t_ref[...], x_ref[...].astype(BF16), preferred_element_type=F32).astype(BF16)


def _fnet_b_kernel(y_ref, g_ref, bcs_ref, wf_ref, o_ref):
    nk, nb = o_ref.shape[0], o_ref.shape[1]
    bc = bcs_ref[:GROUP_W, :]
    bs = bcs_ref[GROUP_W:, :]
    wf = wf_ref[...]
    z = [jnp.dot(g_ref[i], y_ref[i], preferred_element_type=F32) for i in range(nk)]
    f = [jnp.dot(a[:nb].astype(BF16), bc, preferred_element_type=F32)
         + jnp.dot(a[nb:].astype(BF16), bs, preferred_element_type=F32) for a in z]
    for i, a in enumerate(f):
        o_ref[i] = jnp.dot(a.astype(BF16), wf, preferred_element_type=F32).astype(o_ref.dtype)


def _fnet_tables(l):
    na, nb = FFT_NA, l // FFT_NA
    ia = np.arange(na)
    ang_a = 2.0 * np.pi * ((ia[:, None] * ia[None, :]) % na) / na
    ta = np.concatenate([np.cos(ang_a), -np.sin(ang_a)], axis=0) / math.sqrt(na)
    ka = jnp.arange(na, dtype=jnp.int32)[:, None, None]
    kb = jnp.arange(nb, dtype=jnp.int32)[None, :, None]
    n2 = jnp.arange(nb, dtype=jnp.int32)[None, None, :]
    ang = (2.0 * math.pi / l) * ((n2 * (ka + na * kb)) % l).astype(F32)
    cg = jnp.cos(ang) / math.sqrt(nb)
    sg = jnp.sin(ang) / math.sqrt(nb)
    gbig = jnp.concatenate([jnp.concatenate([cg, sg], axis=2),
                            jnp.concatenate([-sg, cg], axis=2)], axis=1).astype(BF16)
    ic = np.arange(GROUP_W)
    same = (ic[:, None] // FNET_BLOCK_W) == (ic[None, :] // FNET_BLOCK_W)
    ang_c = 2.0 * np.pi * (((ic[:, None] % FNET_BLOCK_W) * (ic[None, :] % FNET_BLOCK_W)) % FNET_BLOCK_W) / FNET_BLOCK_W
    bc = np.where(same, np.cos(ang_c), 0.0) / math.sqrt(FNET_BLOCK_W)
    bs = np.where(same, np.sin(ang_c), 0.0) / math.sqrt(FNET_BLOCK_W)
    bcs = np.concatenate([bc, bs], axis=0)
    return jnp.asarray(ta, dtype=F32).astype(BF16), gbig, jnp.asarray(bcs, dtype=F32).astype(BF16)


def _fnet(u, w_fnet, tables):
    b, l, c = u.shape
    na, nb = FFT_NA, l // FFT_NA
    ta, gbig, bcs = tables
    cw = min(2048, nb * c)
    x2 = u.reshape(b, na, nb * c)
    y = pl.pallas_call(
        _fnet_a_kernel,
        grid=(b, nb * c // cw),
        in_specs=[pl.BlockSpec((None, na, cw), lambda bi, j: (bi, 0, j)),
                  pl.BlockSpec((2 * na, na), lambda bi, j: (0, 0))],
        out_specs=pl.BlockSpec((None, 2 * na, cw), lambda bi, j: (bi, 0, j)),
        out_shape=jax.ShapeDtypeStruct((b, 2 * na, nb * c), BF16),
        compiler_params=_cparams(("parallel", "parallel")),
        name="fnet_a",
    )(x2, ta)
    ys = y.reshape(b, 2, na, nb, c).transpose(0, 2, 1, 3, 4).reshape(b, na, 2 * nb, c)
    o = pl.pallas_call(
        _fnet_b_kernel,
        grid=(b, na // FNET_KA_PER_STEP),
        in_specs=[pl.BlockSpec((None, FNET_KA_PER_STEP, 2 * nb, c), lambda bi, k: (bi, k, 0, 0)),
                  pl.BlockSpec((FNET_KA_PER_STEP, 2 * nb, 2 * nb), lambda bi, k: (k, 0, 0)),
                  pl.BlockSpec((2 * c, c), lambda bi, k: (0, 0)),
                  pl.BlockSpec((c, c), lambda bi, k: (0, 0))],
        out_specs=pl.BlockSpec((None, FNET_KA_PER_STEP, nb, c), lambda bi, k: (bi, k, 0, 0)),
        out_shape=jax.ShapeDtypeStruct((b, na, nb, c), BF16),
        compiler_params=_cparams(("parallel", "parallel")),
        name="fnet_b",
    )(ys, gbig, bcs, w_fnet.astype(BF16))
    return o.transpose(0, 2, 1, 3).reshape(b, l, c)


def _gdn_pre_kernel(cur_ref, prev_ref, next_ref, gate_ref, cw_ref, gp_ref, hs_ref, tri_ref, qkv_ref, gb_ref, win_ref):
    i = pl.program_id(1)
    n = pl.num_programs(1)
    tg = cur_ref.shape[0]
    halo = prev_ref.shape[0]
    win_ref[0:halo, :] = prev_ref[...].astype(F32) * (i > 0).astype(F32)
    win_ref[halo:halo + tg, :] = cur_ref[...].astype(F32)
    win_ref[halo + tg:, :] = next_ref[...].astype(F32) * (i < n - 1).astype(F32)
    pad = CONV_K // 2
    acc = cw_ref[0:1, :] * win_ref[pl.ds(halo - pad, tg), :]
    for t in range(1, CONV_K):
        acc = acc + cw_ref[t:t + 1, :] * win_ref[pl.ds(halo - pad + t, tg), :]
    act = _silu(acc)
    hs = hs_ref[...]
    q = act[:, 0:GROUP_W]
    k = act[:, GROUP_W:2 * GROUP_W]
    qn = q * lax.rsqrt(_dot(q * q, hs) + EPS) * (HEAD_DIM ** -0.5)
    kn = k * lax.rsqrt(_dot(k * k, hs) + EPS)
    qkv_ref[:, 0:GROUP_W] = qn.astype(qkv_ref.dtype)
    qkv_ref[:, GROUP_W:2 * GROUP_W] = kn.astype(qkv_ref.dtype)
    qkv_ref[:, 2 * GROUP_W:] = act[:, 2 * GROUP_W:].astype(qkv_ref.dtype)
    a = gate_ref[...]
    x = a + gp_ref[1:2, :]
    softplus = jnp.maximum(x, 0.0) + jnp.log(1.0 + jnp.exp(-jnp.abs(x)))
    g = -jnp.exp(gp_ref[0:1, :]) * softplus
    gcf = _dot_split_left(tri_ref[0], g)
    gcb = _dot_split_left(tri_ref[1], g)
    lane = lax.broadcasted_iota(jnp.int32, a.shape, 1)
    gb_ref[...] = jnp.where(lane < N_HEADS, gcf, jnp.where(lane < 2 * N_HEADS, gcb, _sigmoid(a)))


def _gdn_pre(bqkv, gates, conv_w, a_log, dt_bias, tg):
    b, l, c3 = bqkv.shape
    halo = 2 * SUBLANES
    nt = l // tg
    per = tg // halo
    nh = l // halo
    cw = jnp.zeros((SUBLANES, c3), F32).at[:CONV_K].set(conv_w)
    gp = jnp.zeros((SUBLANES, GATE_PAD), F32)
    gp = gp.at[0, :N_GATE_COLS // 2].set(a_log.reshape(-1)).at[1, :N_GATE_COLS // 2].set(dt_bias.reshape(-1))
    ti = np.arange(tg)
    same_chunk = (ti[:, None] // GDN_CHUNK) == (ti[None, :] // GDN_CHUNK)
    tri = np.stack([same_chunk & (ti[None, :] <= ti[:, None]), same_chunk & (ti[None, :] >= ti[:, None])])
    tri = jnp.asarray(tri.astype(np.float32), dtype=BF16)
    return pl.pallas_call(
        _gdn_pre_kernel,
        grid=(b, nt),
        in_specs=[pl.BlockSpec((None, tg, c3), lambda bi, i: (bi, i, 0)),
                  pl.BlockSpec((None, halo, c3), lambda bi, i: (bi, jnp.maximum(i * per - 1, 0), 0)),
                  pl.BlockSpec((None, halo, c3), lambda bi, i: (bi, jnp.minimum((i + 1) * per, nh - 1), 0)),
                  pl.BlockSpec((None, tg, GATE_PAD), lambda bi, i: (bi, i, 0)),
                  pl.BlockSpec((SUBLANES, c3), lambda bi, i: (0, 0)),
                  pl.BlockSpec((SUBLANES, GATE_PAD), lambda bi, i: (0, 0)),
                  pl.BlockSpec((GROUP_W, GROUP_W), lambda bi, i: (0, 0)),
                  pl.BlockSpec((2, tg, tg), lambda bi, i: (0, 0, 0))],
        out_specs=[pl.BlockSpec((None, tg, c3), lambda bi, i: (bi, i, 0)),
                   pl.BlockSpec((None, tg, GATE_PAD), lambda bi, i: (bi, i, 0))],
        out_shape=[jax.ShapeDtypeStruct((b, l, c3), BF16),
                   jax.ShapeDtypeStruct((b, l, GATE_PAD), F32)],
        scratch_shapes=[pltpu.VMEM((tg + 2 * halo, c3), F32)],
        compiler_params=_cparams(("parallel", "parallel")),
        name="gdn_pre",
    )(bqkv, bqkv, bqkv, gates, cw, gp, _head_sum_matrix(), tri)


def _dot_split_left(m_bf16, a):
    hi = a.astype(BF16)
    lo = (a - hi.astype(F32)).astype(BF16)
    return (jnp.dot(m_bf16, hi, preferred_element_type=F32)
            + jnp.dot(m_bf16, lo, preferred_element_type=F32))


def _block_diag(x, mask01):
    xb = x.astype(BF16)
    return jnp.concatenate([xb] * N_HEADS, axis=0) * mask01


def _gdn_scan_kernel(xf_ref, gbf_ref, xb_ref, gbb_ref, ex_ref, bdm_ref, of_ref, ob_ref,
                     s_ref, u_ref, wq_ref, qk_ref, kdt_ref, el_ref):
    c = GDN_CHUNK
    ng = xf_ref.shape[0] // c
    step = pl.program_id(1)
    slot_w = step % 2
    slot_r = 1 - slot_w

    @pl.when(step == 0)
    def _():
        for ref in (s_ref, u_ref, wq_ref, qk_ref, kdt_ref, el_ref):
            ref[...] = jnp.zeros_like(ref)

    row = lax.broadcasted_iota(jnp.int32, (c, GROUP_W), 0)
    lane = lax.broadcasted_iota(jnp.int32, (c, GROUP_W), 1)
    colm = lane & (HEAD_DIM - 1)
    lane_head = lane >> HEAD_SHIFT
    incl = [(row >= colm), (row <= colm)]
    strict = [(row > colm), (row < colm)]
    eye_f = (row == colm).astype(F32)
    bd = lambda a: _block_diag(a, bdm_ref[0])

    streams = [(g, d) for g in range(ng) for d in range(2)]

    def rows_of(g, d):
        lc = g if d == 0 else ng - 1 - g
        return pl.ds(lc * c, c)

    x_refs = (xf_ref, xb_ref)
    gb_refs = (gbf_ref, gbb_ref)
    o_refs = (of_ref, ob_ref)

    def pre_phase():
        gb = [gb_refs[d][rows_of(g, d), :] for g, d in streams]
        gb_hi = [a.astype(BF16) for a in gb]
        gb_hl = [jnp.concatenate([hi, (a - hi.astype(F32)).astype(BF16)], axis=1) for a, hi in zip(gb, gb_hi)]
        ex_d = [jnp.dot(jnp.concatenate(gb_hl[d::2], axis=0), ex_ref[d], preferred_element_type=F32)
                for d in range(2)]
        ex = [ex_d[d][g * c:(g + 1) * c] for g, d in streams]
        gc = [a[:, :GROUP_W] for a in ex]
        beta = [a[:, GROUP_W:] for a in ex]
        yield
        gbt = [a.T for a in gb]
        gcr = [jnp.concatenate([t[N_HEADS * d + h:N_HEADS * d + h + 1, :] for h in range(N_HEADS)], axis=1)
               for (g, d), t in zip(streams, gbt)]
        decay = [jnp.exp(jnp.where(incl[d], a - r, -jnp.inf)) for (g, d), a, r in zip(streams, gc, gcr)]
        last = [c - 1, 0]
        g_last = [a[last[d]:last[d] + 1, :] for (g, d), a in zip(streams, gc)]
        e_g = [jnp.exp(a) for a in gc]
        e_rem = [jnp.exp(gl - a) for gl, a in zip(g_last, gc)]
        for i, gl in enumerate(g_last):
            el_ref[slot_w, i] = jnp.broadcast_to(jnp.exp(gl), (SUBLANES, GROUP_W))
        yield
        q = [x_refs[d][rows_of(g, d), 0:GROUP_W] for g, d in streams]
        k = [x_refs[d][rows_of(g, d), GROUP_W:2 * GROUP_W] for g, d in streams]
        v = [x_refs[d][rows_of(g, d), 2 * GROUP_W:3 * GROUP_W] for g, d in streams]
        kb = [a * b for a, b in zip(k, beta)]
        kq = [_dot_nt(jnp.concatenate([a, b], axis=0), bd(kk)) for a, b, kk in zip(kb, q, k)]
        lm = [jnp.where(strict[d], a[:c] * dec, 0.0) for (g, d), a, dec in zip(streams, kq, decay)]
        yield
        for i, (a, dec) in enumerate(zip(kq, decay)):
            qk_ref[slot_w, i] = (a[c:] * dec).astype(BF16)
        for i, (a, e) in enumerate(zip(k, e_rem)):
            kdt_ref[slot_w, i] = (a * e).T.astype(BF16)
        minv = [eye_f - jnp.where((row >> 1) == (colm >> 1), a, 0.0) for a in lm]
        lm_b = [a.astype(BF16) for a in lm]
        sh = 1
        while (2 << sh) <= c:
            t = [_dot(m, _block_diag(a, bdm_ref[sh])) for m, a in zip(minv, lm_b)]
            yield
            t = [_dot(a, bd(m)) for a, m in zip(t, minv)]
            minv = [m - a for m, a in zip(minv, t)]
            yield
            sh += 1
        for i, (m, a, b) in enumerate(zip(minv, v, beta)):
            u_ref[slot_w, i] = _dot(m, bd(a * b))
        yield
        for i, (m, a, e, qq) in enumerate(zip(minv, kb, e_g, q)):
            wq_ref[slot_w, i] = jnp.concatenate([_dot(m, bd(a * e)), qq * e], axis=0).astype(BF16)

    def state_phase():
        for g in range(ng):
            idx = [2 * g, 2 * g + 1]
            state = [s_ref[d] for d in range(2)]
            ws = [jnp.dot(wq_ref[slot_r, i], bd(s), preferred_element_type=F32) for i, s in zip(idx, state)]
            yield
            v_new = [u_ref[slot_r, i] - a[:c] for i, a in zip(idx, ws)]
            o2 = [jnp.dot(qk_ref[slot_r, i], bd(a), preferred_element_type=F32) for i, a in zip(idx, v_new)]
            upd = [jnp.dot(kdt_ref[slot_r, i], a.astype(BF16), preferred_element_type=F32)
                   for i, a in zip(idx, v_new)]
            for d in range(2):
                o_refs[d][rows_of(g, d), :] = (ws[d][c:] + o2[d]).astype(o_refs[d].dtype)
                fold = jnp.where(lane_head == 0, upd[d][0:c], 0.0)
                for h in range(1, N_HEADS):
                    fold = fold + jnp.where(lane_head == h, upd[d][h * c:(h + 1) * c], 0.0)
                s_ref[d] = state[d] * el_ref[slot_r, idx[d], 0:1, :] + fold
            yield

    phases = [pre_phase(), state_phase()]
    while phases:
        for p in list(phases):
            if next(p, StopIteration) is StopIteration:
                phases.remove(p)


def _gate_expanders():
    e = np.zeros((2, 2 * GATE_PAD, 2 * GROUP_W), np.float32)
    for d in range(2):
        for h in range(N_HEADS):
            for half in range(2):
                r0 = half * GATE_PAD
                e[d, r0 + N_HEADS * d + h, h * HEAD_DIM:(h + 1) * HEAD_DIM] = 1.0
                e[d, r0 + N_GATE_COLS // 2 + N_HEADS * d + h, GROUP_W + h * HEAD_DIM:GROUP_W + (h + 1) * HEAD_DIM] = 1.0
    return jnp.asarray(e, dtype=BF16)


def _block_diag_masks():
    i = np.arange(GROUP_W)
    r, cc = i[:, None], i[None, :]
    same_head = (r >> HEAD_SHIFT) == (cc >> HEAD_SHIFT)
    rl, cl = r & (HEAD_DIM - 1), cc & (HEAD_DIM - 1)
    masks = [same_head]
    sh = 1
    while (2 << sh) <= HEAD_DIM:
        masks.append(same_head & ((rl >> (sh + 1)) == (cl >> (sh + 1))) & ((rl >> sh) != (cl >> sh)))
        sh += 1
    return jnp.asarray(np.stack(masks).astype(np.float32), dtype=BF16)


def _gdn_scan(qkvn, gb, ng):
    b, l, c3 = qkvn.shape
    rows = ng * GDN_CHUNK
    nsteps = l // rows
    grp_in = lambda i: jnp.minimum(i, nsteps - 1)
    grp_out = lambda i: jnp.maximum(i - 1, 0)
    fwd = lambda w, f: pl.BlockSpec((None, rows, w), lambda bi, i: (bi, f(i), 0))
    bwd = lambda w, f: pl.BlockSpec((None, rows, w), lambda bi, i: (bi, nsteps - 1 - f(i), 0))
    ns = 2 * ng
    c = GDN_CHUNK
    masks = _block_diag_masks()
    return pl.pallas_call(
        _gdn_scan_kernel,
        grid=(b, nsteps + 1),
        in_specs=[fwd(c3, grp_in), fwd(GATE_PAD, grp_in), bwd(c3, grp_in), bwd(GATE_PAD, grp_in),
                  pl.BlockSpec((2, 2 * GATE_PAD, 2 * GROUP_W), lambda bi, i: (0, 0, 0)),
                  pl.BlockSpec(masks.shape, lambda bi, i: (0, 0, 0))],
        out_specs=[fwd(GROUP_W, grp_out), bwd(GROUP_W, grp_out)],
        out_shape=[jax.ShapeDtypeStruct((b, l, GROUP_W), BF16)] * 2,
        scratch_shapes=[pltpu.VMEM((2, HEAD_DIM, GROUP_W), F32),
                        pltpu.VMEM((2, ns, c, GROUP_W), F32),
                        pltpu.VMEM((2, ns, 2 * c, GROUP_W), BF16),
                        pltpu.VMEM((2, ns, c, GROUP_W), BF16),
                        pltpu.VMEM((2, ns, GROUP_W, c), BF16),
                        pltpu.VMEM((2, ns, SUBLANES, GROUP_W), F32)],
        compiler_params=_cparams(("parallel", "arbitrary")),
        name="gdn_scan",
    )(qkvn, gb, qkvn, gb, _gate_expanders(), masks)


def _head_stack(q):
    lane_head = lax.broadcasted_iota(jnp.int32, q.shape, 1) >> HEAD_SHIFT
    zero = jnp.zeros((), q.dtype)
    return jnp.concatenate([jnp.where(lane_head == h, q, zero) for h in range(N_HEADS)], axis=0)


def _head_unstack(pv, m):
    lane_head = lax.broadcasted_iota(jnp.int32, (m, GROUP_W), 1) >> HEAD_SHIFT
    out = jnp.where(lane_head == 0, pv[0:m], 0.0)
    for h in range(1, N_HEADS):
        out = out + jnp.where(lane_head == h, pv[h * m:(h + 1) * m], 0.0)
    return out


def _attend(qs, ks, vs, biases):
    s = [_dot_nt(a, b) for a, b in zip(qs, ks)]
    s = [a if b is None else a + b for a, b in zip(s, biases)]
    m = [jnp.max(a, axis=-1, keepdims=True) for a in s]
    p = [jnp.exp(a - b) for a, b in zip(s, m)]
    den = [jnp.sum(a, axis=-1, keepdims=True) for a in p]
    pv = [jnp.dot(a.astype(BF16), b, preferred_element_type=F32) for a, b in zip(p, vs)]
    return [a / b for a, b in zip(pv, den)]


def _natten_kernel(q_ref, kp_ref, kc_ref, kn_ref, vp_ref, vc_ref, vn_ref, bias_ref, o_ref, kwin_ref, vwin_ref):
    gi = pl.program_id(1)
    rows = pl.num_programs(1) * NA_KH
    blk = kc_ref.shape[0]
    for t, (kr, vr) in enumerate(((kp_ref, vp_ref), (kc_ref, vc_ref), (kn_ref, vn_ref))):
        kwin_ref[t * blk:(t + 1) * blk, :] = kr[...]
        vwin_ref[t * blk:(t + 1) * blk, :] = vr[...]

    def body(t, carry):
        js = [t * NA_ROWS_PER_ITER + i for i in range(NA_ROWS_PER_ITER)]
        r = [gi * NA_KH + j for j in js]
        rs = [jnp.clip(a - NA_KH // 2, 0, rows - NA_KH) for a in r]
        start = [pl.multiple_of((a - (gi - 1) * NA_KH) * GRID_W, GRID_W) for a in rs]
        qrows = [pl.ds(pl.multiple_of(j * GRID_W, GRID_W), GRID_W) for j in js]
        qs = [_head_stack(q_ref[a, :] * (HEAD_DIM ** -0.5)) for a in qrows]
        kw = [kwin_ref[pl.ds(a, NA_KH * GRID_W), :] for a in start]
        vw = [vwin_ref[pl.ds(a, NA_KH * GRID_W), :] for a in start]
        bias = [bias_ref[a - b] for a, b in zip(r, rs)]
        for a, pv in zip(qrows, _attend(qs, kw, vw, bias)):
            o_ref[a, :] = _head_unstack(pv, GRID_W).astype(o_ref.dtype)
        return carry

    lax.fori_loop(0, NA_KH // NA_ROWS_PER_ITER, body, 0)


def _natten_bias(rpb):
    w = np.arange(GRID_W)
    cs = np.clip(w - NA_KW // 2, 0, GRID_W - NA_KW)
    wk = np.arange(GRID_W)
    in_win = (wk[None, :] >= cs[:, None]) & (wk[None, :] < cs[:, None] + NA_KW)
    col_off = wk[None, :] - w[:, None] + NA_KW - 1
    delta = np.arange(NA_KH)
    i = np.arange(NA_KH)
    row_off = i[None, :] - delta[:, None] + NA_KH - 1
    sel_r = (row_off[:, :, None] == np.arange(2 * NA_KH - 1)).astype(np.float32)
    sel_c = ((col_off[:, :, None] == np.arange(2 * NA_KW - 1)) & in_win[:, :, None]).astype(np.float32)
    t = jnp.einsum('hab,dia,wvb->hdiwv', rpb.astype(F32), sel_r, sel_c, precision=lax.Precision.HIGHEST)
    t = jnp.where(jnp.asarray(in_win)[None, None, None], t, NEG_BIG)
    t = t.transpose(1, 0, 3, 2, 4)
    return t.reshape(NA_KH, N_HEADS * GRID_W, NA_KH * GRID_W).astype(F32)


def _natten(cqkv, bias):
    b, l, _ = cqkv.shape
    blk = NA_KH * GRID_W
    ng = l // blk
    spec = lambda colblk, f: pl.BlockSpec((None, blk, GROUP_W), lambda bi, g: (bi, f(g), colblk))
    cur = lambda g: g
    prev = lambda g: jnp.maximum(g - 1, 0)
    nxt = lambda g: jnp.minimum(g + 1, ng - 1)
    return pl.pallas_call(
        _natten_kernel,
        grid=(b, ng),
        in_specs=[spec(0, cur), spec(1, prev), spec(1, cur), spec(1, nxt),
                  spec(2, prev), spec(2, cur), spec(2, nxt),
                  pl.BlockSpec(bias.shape, lambda bi, g: (0, 0, 0))],
        out_specs=pl.BlockSpec((None, blk, GROUP_W), lambda bi, g: (bi, g, 0)),
        out_shape=jax.ShapeDtypeStruct((b, l, GROUP_W), BF16),
        scratch_shapes=[pltpu.VMEM((3 * blk, GROUP_W), BF16), pltpu.VMEM((3 * blk, GROUP_W), BF16)],
        compiler_params=_cparams(("parallel", "parallel")),
        name="natten",
    )(cqkv, cqkv, cqkv, cqkv, cqkv, cqkv, cqkv, bias)


def _memattn(q_ref, k_ref, v_ref):
    ts = q_ref.shape[0] // MEM_Q_SPLIT
    parts = [pl.ds(i * ts, ts) for i in range(MEM_Q_SPLIT)]
    qs = [_head_stack(q_ref[a, :] * (HEAD_DIM ** -0.5)) for a in parts]
    k = k_ref[...]
    v = v_ref[...]
    pv = _attend(qs, [k] * MEM_Q_SPLIT, [v] * MEM_Q_SPLIT, [None] * MEM_Q_SPLIT)
    return [_head_unstack(a, ts) for a in pv]


def _outproj_kernel(ya_ref, of_ref, ob_ref, yc_ref, dq_ref, z_ref, x_ref, mem_ref, mg_ref, wkv_ref,
                    gg_ref, pg_ref, hs_ref, w_ref, o_ref, k_ref, v_ref):
    @pl.when(pl.program_id(1) == 0)
    def _():
        m = mem_ref[...]
        ms = jnp.mean(m * m, axis=-1, keepdims=True)
        mn = (m * lax.rsqrt(ms + EPS) * mg_ref[...]).astype(BF16)
        kv = jnp.dot(mn, wkv_ref[...], preferred_element_type=F32)
        k_ref[...] = kv[:, :GROUP_W].astype(BF16)
        v_ref[...] = kv[:, GROUP_W:].astype(BF16)

    yd = jnp.concatenate(_memattn(dq_ref, k_ref, v_ref), axis=0)
    o = of_ref[...].astype(F32) + ob_ref[...].astype(F32)
    ms = _dot(o * o, hs_ref[...]) * (1.0 / HEAD_DIM)
    yb = o * lax.rsqrt(ms + EPS) * gg_ref[...]
    acc = None
    for i, y in enumerate((ya_ref[...], yb, yc_ref[...], yd)):
        gated = (y.astype(F32) * _silu(z_ref[:, i * GROUP_W:(i + 1) * GROUP_W].astype(F32))).astype(BF16)
        part = jnp.dot(gated, w_ref[i * GROUP_W:(i + 1) * GROUP_W, :], preferred_element_type=F32)
        acc = part if acc is None else acc + part
    ms = jnp.mean(acc * acc, axis=-1, keepdims=True)
    o_ref[...] = x_ref[...] + acc * lax.rsqrt(ms + EPS) * pg_ref[...]


def _outproj(ya, o_f, o_b, yc, dq, z, x, mem, mem_g, w_mem_kv, gdn_g, post_g, w_out, tm):
    b, l, d = x.shape
    m = mem.shape[1]
    grp = pl.BlockSpec((None, tm, GROUP_W), lambda bi, i: (bi, i, 0))
    full = pl.BlockSpec((None, tm, d), lambda bi, i: (bi, i, 0))
    const = lambda shape: pl.BlockSpec(shape, lambda bi, i: (0, 0))
    return pl.pallas_call(
        _outproj_kernel,
        grid=(b, l // tm),
        in_specs=[grp, grp, grp, grp, grp, full, full,
                  pl.BlockSpec((None, m, d), lambda bi, i: (bi, 0, 0)), const((1, d)), const((d, 2 * GROUP_W)),
                  const((1, GROUP_W)), const((1, d)), const((GROUP_W, GROUP_W)), const((d, d))],
        out_specs=full,
        out_shape=jax.ShapeDtypeStruct((b, l, d), F32),
        scratch_shapes=[pltpu.VMEM((m, GROUP_W), BF16), pltpu.VMEM((m, GROUP_W), BF16)],
        compiler_params=_cparams(("parallel", "arbitrary")),
        name="outproj",
    )(ya, o_f, o_b, yc, dq, z, x, mem, mem_g.reshape(1, d), w_mem_kv.astype(BF16),
      jnp.tile(gdn_g, N_HEADS).reshape(1, GROUP_W), post_g.reshape(1, d), _head_sum_matrix(), w_out.astype(BF16))


def _tile(l, want):
    t = min(want, l)
    assert l % t == 0
    return t


def _layer(x, mem, tables, pre_g, post_g, w_perm, w_fnet, conv_w, a_log, dt_bias, gdn_g, na_bias, mem_g, w_mem_kv,
           w_out):
    l = x.shape[1]
    u, bqkv, bgate, cqkv, dq, z = _inproj(x, pre_g, w_perm, _tile(l, 1024))
    ya = _fnet(u, w_fnet, tables)
    qkvn, gb = _gdn_pre(bqkv, bgate, conv_w, a_log, dt_bias, _tile(l, 512))
    o_f, o_b = _gdn_scan(qkvn, gb, GDN_CHUNKS_PER_STEP)
    yc = _natten(cqkv, na_bias)
    return _outproj(ya, o_f, o_b, yc, dq, z, x, mem, mem_g, w_mem_kv, gdn_g, post_g, w_out, _tile(l, 512))


def kernel(x_prompt, x_sample, mem_prompt, mem_sample, pre_norm_g, post_norm_g, w_in, w_fnet, gdn_conv_w,
           gdn_a_log, gdn_dt_bias, gdn_norm_g, na_rpb, mem_norm_g, w_mem_kv, w_out):
    l = x_prompt.shape[1]
    assert x_prompt.shape[1:] == x_sample.shape[1:] == (l, D_MODEL)
    assert l % (NA_KH * GRID_W) == 0 and l % (GDN_CHUNKS_PER_STEP * GDN_CHUNK) == 0 and l % FFT_NA == 0
    depth = pre_norm_g.shape[0]
    tables = _fnet_tables(l)
    w_perm = [_permute_w_in(w_in[i]) for i in range(depth)]
    na_bias = [_natten_bias(na_rpb[i]) for i in range(depth)]

    def trunk(x, mem):
        for i in range(depth):
            x = _layer(x, mem, tables, pre_norm_g[i], post_norm_g[i], w_perm[i], w_fnet[i], gdn_conv_w[i],
                       gdn_a_log[i], gdn_dt_bias[i], gdn_norm_g[i], na_bias[i], mem_norm_g[i], w_mem_kv[i], w_out[i])
        return x

    return (trunk(x_prompt, mem_prompt), trunk(x_sample, mem_sample))
```

```python
import functools
import math

import numpy as np
import jax
import jax.numpy as jnp
from jax import lax
from jax.experimental import pallas as pl
from jax.experimental.pallas import tpu as pltpu

F32 = jnp.float32
BF16 = jnp.bfloat16

D_MODEL = 1024
GROUP_W = 256
HEAD_DIM = 64
HEAD_SHIFT = 6
N_HEADS = 4
FNET_BLOCK_W = 64
GDN_CHUNK = 64
CONV_K = 5
GRID_W = 64
NA_KH = 8
NA_KW = 16
N_GATE_COLS = 16
EPS = 1e-6
NEG_BIG = -1e30

LANES = 128
SUBLANES = 8
GATE_PAD = LANES
D_PROJ = GROUP_W + 3 * GROUP_W + GATE_PAD + 3 * GROUP_W + GROUP_W + 4 * GROUP_W
FFT_NA = 64
FNET_KA_PER_STEP = 8
GDN_CHUNKS_PER_STEP = 4
NA_ROWS_PER_ITER = 4
MEM_Q_SPLIT = 4
VMEM_LIMIT = 56 * 1024 * 1024


def _cparams(sem):
    return pltpu.CompilerParams(dimension_semantics=sem, vmem_limit_bytes=VMEM_LIMIT)


def _dot(a, b):
    return jnp.dot(a.astype(BF16), b.astype(BF16), preferred_element_type=F32)


def _dot_nt(a, b):
    return lax.dot_general(a.astype(BF16), b.astype(BF16), (((1,), (1,)), ((), ())),
                           preferred_element_type=F32)


def _dot_split(a, b_bf16):
    hi = a.astype(BF16)
    lo = (a - hi.astype(F32)).astype(BF16)
    return (jnp.dot(hi, b_bf16, preferred_element_type=F32)
            + jnp.dot(lo, b_bf16, preferred_element_type=F32))


def _sigmoid(x):
    return 1.0 / (1.0 + jnp.exp(-x))


def _silu(x):
    return x * _sigmoid(x)


def _head_sum_matrix():
    idx = np.arange(GROUP_W) // HEAD_DIM
    return jnp.asarray((idx[:, None] == idx[None, :]).astype(np.float32), dtype=BF16)


def _inproj_kernel(x_ref, xp_ref, xn_ref, g_ref, w_ref, cw_ref, gp_ref, hs_ref, tri_ref,
                   u_ref, qkv_ref, gb_ref, cqkv_ref, dq_ref, z_ref, win_ref):
    i = pl.program_id(1)
    n = pl.num_programs(1)
    tm = x_ref.shape[0]
    halo = xp_ref.shape[0]
    o_bqkv, o_gate, o_cqkv = GROUP_W, 4 * GROUP_W, 4 * GROUP_W + GATE_PAD
    o_dq, o_z = o_cqkv + 3 * GROUP_W, o_cqkv + 4 * GROUP_W

    def normed(x):
        ms = jnp.mean(x * x, axis=-1, keepdims=True)
        return (x * lax.rsqrt(ms + EPS) * g_ref[...]).astype(BF16)

    def proj(hh, lo, hi):
        return jnp.dot(hh, w_ref[:, lo:hi], preferred_element_type=F32)

    h = normed(x_ref[...])
    h_halo = normed(jnp.concatenate([xp_ref[...], xn_ref[...]], axis=0))
    bq = proj(jnp.concatenate([h, h_halo], axis=0), o_bqkv, o_gate)
    win_ref[halo:halo + tm, :] = bq[:tm]
    win_ref[0:halo, :] = bq[tm:tm + halo] * (i > 0).astype(F32)
    win_ref[halo + tm:, :] = bq[tm + halo:] * (i < n - 1).astype(F32)
    a = proj(h, o_gate, o_cqkv)
    pad = CONV_K // 2
    hs = hs_ref[...]
    ts = tri_ref.shape[1]
    lane = lax.broadcasted_iota(jnp.int32, (ts, GATE_PAD), 1)

    def gdn_front(j):
        rows = pl.ds(j * ts, ts)
        acc = cw_ref[0:1, :] * win_ref[pl.ds(halo - pad + j * ts, ts), :]
        for t in range(1, CONV_K):
            acc = acc + cw_ref[t:t + 1, :] * win_ref[pl.ds(halo - pad + t + j * ts, ts), :]
        act = _silu(acc)
        q = act[:, 0:GROUP_W]
        k = act[:, GROUP_W:2 * GROUP_W]
        qkv_ref[rows, 0:GROUP_W] = (q * lax.rsqrt(_dot(q * q, hs) + EPS) * (HEAD_DIM ** -0.5)).astype(qkv_ref.dtype)
        qkv_ref[rows, GROUP_W:2 * GROUP_W] = (k * lax.rsqrt(_dot(k * k, hs) + EPS)).astype(qkv_ref.dtype)
        qkv_ref[rows, 2 * GROUP_W:] = act[:, 2 * GROUP_W:].astype(qkv_ref.dtype)
        aj = a[j * ts:(j + 1) * ts]
        xg = aj + gp_ref[1:2, :]
        softplus = jnp.maximum(xg, 0.0) + jnp.log(1.0 + jnp.exp(-jnp.abs(xg)))
        g = -jnp.exp(gp_ref[0:1, :]) * softplus
        gcf = _dot_split_left(tri_ref[0], g)
        gcb = _dot_split_left(tri_ref[1], g)
        gb_ref[rows, :] = jnp.where(lane < N_HEADS, gcf, jnp.where(lane < 2 * N_HEADS, gcb, _sigmoid(aj)))

    def store(ref, lo, hi, col=None):
        val = proj(h, lo, hi).astype(ref.dtype)
        if col is None:
            ref[...] = val
        else:
            ref[:, col:col + hi - lo] = val

    pieces = [lambda: store(cqkv_ref, o_cqkv, o_dq),
              lambda: (store(dq_ref, o_dq, o_z), store(u_ref, 0, o_bqkv)),
              lambda: store(z_ref, o_z, o_z + 2 * GROUP_W, 0),
              lambda: store(z_ref, o_z + 2 * GROUP_W, o_z + 4 * GROUP_W, 2 * GROUP_W)]
    nsub = tm // ts
    for j in range(max(nsub, len(pieces))):
        if j < len(pieces):
            pieces[j]()
        if j < nsub:
            gdn_front(j)


def _permute_w_in(w_in):
    g = GROUP_W
    o_az, o_bqkv, o_bz, o_bg = g, 2 * g, 5 * g, 6 * g
    o_cqkv = o_bg + N_GATE_COLS
    o_cz, o_dq, o_dz = o_cqkv + 3 * g, o_cqkv + 4 * g, o_cqkv + 5 * g
    pad = jnp.zeros((D_MODEL, GATE_PAD - N_GATE_COLS), w_in.dtype)
    cols = [w_in[:, 0:g], w_in[:, o_bqkv:o_bz], w_in[:, o_bg:o_cqkv], pad, w_in[:, o_cqkv:o_cz],
            w_in[:, o_dq:o_dz], w_in[:, o_az:o_bqkv], w_in[:, o_bz:o_bg], w_in[:, o_cz:o_dq],
            w_in[:, o_dz:o_dz + g]]
    return jnp.concatenate(cols, axis=1).astype(BF16)


def _inproj(x, pre_g, w_perm, conv_w, a_log, dt_bias, tm):
    b, l, d = x.shape
    widths = (GROUP_W, 3 * GROUP_W, GATE_PAD, 3 * GROUP_W, GROUP_W, 4 * GROUP_W)
    dtypes = (BF16, BF16, F32, BF16, BF16, BF16)
    c3 = 3 * GROUP_W
    halo = SUBLANES
    per = tm // halo
    nh = l // halo
    ts = min(tm, 4 * GDN_CHUNK)
    cw = jnp.zeros((SUBLANES, c3), F32).at[:CONV_K].set(conv_w)
    gp = jnp.zeros((SUBLANES, GATE_PAD), F32)
    gp = gp.at[0, :N_GATE_COLS // 2].set(a_log.reshape(-1)).at[1, :N_GATE_COLS // 2].set(dt_bias.reshape(-1))
    ti = np.arange(ts)
    same_chunk = (ti[:, None] // GDN_CHUNK) == (ti[None, :] // GDN_CHUNK)
    tri = np.stack([same_chunk & (ti[None, :] <= ti[:, None]), same_chunk & (ti[None, :] >= ti[:, None])])
    tri = jnp.asarray(tri.astype(np.float32), dtype=BF16)
    row = lambda n: pl.BlockSpec((None, tm, n), lambda bi, i: (bi, i, 0))
    const = lambda shape: pl.BlockSpec(shape, lambda bi, i: (0,) * len(shape))
    return pl.pallas_call(
        _inproj_kernel,
        grid=(b, l // tm),
        in_specs=[row(d),
                  pl.BlockSpec((None, halo, d), lambda bi, i: (bi, jnp.maximum(i * per - 1, 0), 0)),
                  pl.BlockSpec((None, halo, d), lambda bi, i: (bi, jnp.minimum((i + 1) * per, nh - 1), 0)),
                  const((1, d)), const((d, D_PROJ)), const((SUBLANES, c3)), const((SUBLANES, GATE_PAD)),
                  const((GROUP_W, GROUP_W)), const((2, ts, ts))],
        out_specs=[row(n) for n in widths],
        out_shape=[jax.ShapeDtypeStruct((b, l, n), dt) for n, dt in zip(widths, dtypes)],
        scratch_shapes=[pltpu.VMEM((tm + 2 * halo, c3), F32)],
        compiler_params=_cparams(("parallel", "parallel")),
        name="inproj",
    )(x, x, x, pre_g.reshape(1, d), w_perm, cw, gp, _head_sum_matrix(), tri)


def _fnet_a_kernel(x_ref, t_ref, y_ref):
    y_ref[...] = jnp.dot(t_ref[...], x_ref[...].astype(BF16), preferred_element_type=F32).astype(BF16)


def _fnet_b_kernel(y_ref, g_ref, bcs_ref, wf_ref, o_ref):
    nk, nb = o_ref.shape[0], o_ref.shape[1]
    bc = bcs_ref[:GROUP_W, :]
    bs = bcs_ref[GROUP_W:, :]
    wf = wf_ref[...]
    z = [jnp.dot(g_ref[i], y_ref[i], preferred_element_type=F32) for i in range(nk)]
    f = [jnp.dot(a[:nb].astype(BF16), bc, preferred_element_type=F32)
         + jnp.dot(a[nb:].astype(BF16), bs, preferred_element_type=F32) for a in z]
    for i, a in enumerate(f):
        o_ref[i] = jnp.dot(a.astype(BF16), wf, preferred_element_type=F32).astype(o_ref.dtype)


def _fnet_tables(l):
    na, nb = FFT_NA, l // FFT_NA
    ia = np.arange(na)
    ang_a = 2.0 * np.pi * ((ia[:, None] * ia[None, :]) % na) / na
    ta = np.concatenate([np.cos(ang_a), -np.sin(ang_a)], axis=0) / math.sqrt(na)
    ka = jnp.arange(na, dtype=jnp.int32)[:, None, None]
    kb = jnp.arange(nb, dtype=jnp.int32)[None, :, None]
    n2 = jnp.arange(nb, dtype=jnp.int32)[None, None, :]
    ang = (2.0 * math.pi / l) * ((n2 * (ka + na * kb)) % l).astype(F32)
    cg = jnp.cos(ang) / math.sqrt(nb)
    sg = jnp.sin(ang) / math.sqrt(nb)
    gbig = jnp.concatenate([jnp.concatenate([cg, sg], axis=2),
                            jnp.concatenate([-sg, cg], axis=2)], axis=1).astype(BF16)
    ic = np.arange(GROUP_W)
    same = (ic[:, None] // FNET_BLOCK_W) == (ic[None, :] // FNET_BLOCK_W)
    ang_c = 2.0 * np.pi * (((ic[:, None] % FNET_BLOCK_W) * (ic[None, :] % FNET_BLOCK_W)) % FNET_BLOCK_W) / FNET_BLOCK_W
    bc = np.where(same, np.cos(ang_c), 0.0) / math.sqrt(FNET_BLOCK_W)
    bs = np.where(same, np.sin(ang_c), 0.0) / math.sqrt(FNET_BLOCK_W)
    bcs = np.concatenate([bc, bs], axis=0)
    return jnp.asarray(ta, dtype=F32).astype(BF16), gbig, jnp.asarray(bcs, dtype=F32).astype(BF16)


def _fnet(u, w_fnet, tables):
    b, l, c = u.shape
    na, nb = FFT_NA, l // FFT_NA
    ta, gbig, bcs = tables
    cw = min(8192, nb * c)
    x2 = u.reshape(b, na, nb * c)
    y = pl.pallas_call(
        _fnet_a_kernel,
        grid=(b, nb * c // cw),
        in_specs=[pl.BlockSpec((None, na, cw), lambda bi, j: (bi, 0, j)),
                  pl.BlockSpec((2 * na, na), lambda bi, j: (0, 0))],
        out_specs=pl.BlockSpec((None, 2 * na, cw), lambda bi, j: (bi, 0, j)),
        out_shape=jax.ShapeDtypeStruct((b, 2 * na, nb * c), BF16),
        compiler_params=_cparams(("parallel", "parallel")),
        name="fnet_a",
    )(x2, ta)
    ys = y.reshape(b, 2, na, nb, c).transpose(0, 2, 1, 3, 4).reshape(b, na, 2 * nb, c)
    o = pl.pallas_call(
        _fnet_b_kernel,
        grid=(b, na // FNET_KA_PER_STEP),
        in_specs=[pl.BlockSpec((None, FNET_KA_PER_STEP, 2 * nb, c), lambda bi, k: (bi, k, 0, 0)),
                  pl.BlockSpec((FNET_KA_PER_STEP, 2 * nb, 2 * nb), lambda bi, k: (k, 0, 0)),
                  pl.BlockSpec((2 * c, c), lambda bi, k: (0, 0)),
                  pl.BlockSpec((c, c), lambda bi, k: (0, 0))],
        out_specs=pl.BlockSpec((None, FNET_KA_PER_STEP, nb, c), lambda bi, k: (bi, k, 0, 0)),
        out_shape=jax.ShapeDtypeStruct((b, na, nb, c), BF16),
        compiler_params=_cparams(("parallel", "parallel")),
        name="fnet_b",
    )(ys, gbig, bcs, w_fnet.astype(BF16))
    return o.transpose(0, 2, 1, 3).reshape(b, l, c)


def _dot_split_left(m_bf16, a):
    hi = a.astype(BF16)
    lo = (a - hi.astype(F32)).astype(BF16)
    return (jnp.dot(m_bf16, hi, preferred_element_type=F32)
            + jnp.dot(m_bf16, lo, preferred_element_type=F32))


def _block_diag(x, mask01):
    xb = x.astype(BF16)
    return jnp.concatenate([xb] * N_HEADS, axis=0) * mask01


def _gdn_scan_kernel(xf_ref, gbf_ref, xb_ref, gbb_ref, ex_ref, bdm_ref, of_ref, ob_ref,
                     s_ref, u_ref, wq_ref, qk_ref, kdt_ref, el_ref):
    c = GDN_CHUNK
    ng = xf_ref.shape[0] // c
    step = pl.program_id(1)
    slot_w = step % 2
    slot_r = 1 - slot_w

    @pl.when(step == 0)
    def _():
        for ref in (s_ref, u_ref, wq_ref, qk_ref, kdt_ref, el_ref):
            ref[...] = jnp.zeros_like(ref)

    row = lax.broadcasted_iota(jnp.int32, (c, GROUP_W), 0)
    lane = lax.broadcasted_iota(jnp.int32, (c, GROUP_W), 1)
    colm = lane & (HEAD_DIM - 1)
    lane_head = lane >> HEAD_SHIFT
    incl = [(row >= colm), (row <= colm)]
    strict = [(row > colm), (row < colm)]
    eye_f = (row == colm).astype(F32)
    bd = lambda a: _block_diag(a, bdm_ref[0])

    streams = [(g, d) for g in range(ng) for d in range(2)]

    def rows_of(g, d):
        lc = g if d == 0 else ng - 1 - g
        return pl.ds(lc * c, c)

    x_refs = (xf_ref, xb_ref)
    gb_refs = (gbf_ref, gbb_ref)
    o_refs = (of_ref, ob_ref)

    def pre_phase():
        gb = [gb_refs[d][rows_of(g, d), :] for g, d in streams]
        gb_hi = [a.astype(BF16) for a in gb]
        gb_hl = [jnp.concatenate([hi, (a - hi.astype(F32)).astype(BF16)], axis=1) for a, hi in zip(gb, gb_hi)]
        ex_d = [jnp.dot(jnp.concatenate(gb_hl[d::2], axis=0), ex_ref[d], preferred_element_type=F32)
                for d in range(2)]
        ex = [ex_d[d][g * c:(g + 1) * c] for g, d in streams]
        gc = [a[:, :GROUP_W] for a in ex]
        beta = [a[:, GROUP_W:] for a in ex]
        yield
        gbt = [a.T for a in gb]
        gcr = [jnp.concatenate([t[N_HEADS * d + h:N_HEADS * d + h + 1, :] for h in range(N_HEADS)], axis=1)
               for (g, d), t in zip(streams, gbt)]
        decay = [jnp.exp(jnp.where(incl[d], a - r, -jnp.inf)) for (g, d), a, r in zip(streams, gc, gcr)]
        last = [c - 1, 0]
        g_last = [a[last[d]:last[d] + 1, :] for (g, d), a in zip(streams, gc)]
        e_g = [jnp.exp(a) for a in gc]
        e_rem = [jnp.exp(gl - a) for gl, a in zip(g_last, gc)]
        for i, gl in enumerate(g_last):
            el_ref[slot_w, i] = jnp.broadcast_to(jnp.exp(gl), (SUBLANES, GROUP_W))
        yield
        q = [x_refs[d][rows_of(g, d), 0:GROUP_W] for g, d in streams]
        k = [x_refs[d][rows_of(g, d), GROUP_W:2 * GROUP_W] for g, d in streams]
        v = [x_refs[d][rows_of(g, d), 2 * GROUP_W:3 * GROUP_W] for g, d in streams]
        kb = [a * b for a, b in zip(k, beta)]
        kq = [_dot_nt(jnp.concatenate([a, b], axis=0), bd(kk)) for a, b, kk in zip(kb, q, k)]
        lm = [jnp.where(strict[d], a[:c] * dec, 0.0) for (g, d), a, dec in zip(streams, kq, decay)]
        yield
        for i, (a, dec) in enumerate(zip(kq, decay)):
            qk_ref[slot_w, i] = (a[c:] * dec).astype(BF16)
        for i, (a, e) in enumerate(zip(k, e_rem)):
            kdt_ref[slot_w, i] = (a * e).T.astype(BF16)
        minv = [eye_f - jnp.where((row >> 1) == (colm >> 1), a, 0.0) for a in lm]
        lm_b = [a.astype(BF16) for a in lm]
        sh = 1
        while (2 << sh) <= c:
            t = [_dot(m, _block_diag(a, bdm_ref[sh])) for m, a in zip(minv, lm_b)]
            yield
            t = [_dot(a, bd(m)) for a, m in zip(t, minv)]
            minv = [m - a for m, a in zip(minv, t)]
            yield
            sh += 1
        for i, (m, a, b) in enumerate(zip(minv, v, beta)):
            u_ref[slot_w, i] = _dot(m, bd(a * b))
        yield
        for i, (m, a, e, qq) in enumerate(zip(minv, kb, e_g, q)):
            wq_ref[slot_w, i] = jnp.concatenate([_dot(m, bd(a * e)), qq * e], axis=0).astype(BF16)

    def state_phase():
        for g in range(ng):
            idx = [2 * g, 2 * g + 1]
            state = [s_ref[d] for d in range(2)]
            ws = [jnp.dot(wq_ref[slot_r, i], bd(s), preferred_element_type=F32) for i, s in zip(idx, state)]
            yield
            v_new = [u_ref[slot_r, i] - a[:c] for i, a in zip(idx, ws)]
            o2 = [jnp.dot(qk_ref[slot_r, i], bd(a), preferred_element_type=F32) for i, a in zip(idx, v_new)]
            upd = [jnp.dot(kdt_ref[slot_r, i], a.astype(BF16), preferred_element_type=F32)
                   for i, a in zip(idx, v_new)]
            for d in range(2):
                o_refs[d][rows_of(g, d), :] = (ws[d][c:] + o2[d]).astype(o_refs[d].dtype)
                fold = jnp.where(lane_head == 0, upd[d][0:c], 0.0)
                for h in range(1, N_HEADS):
                    fold = fold + jnp.where(lane_head == h, upd[d][h * c:(h + 1) * c], 0.0)
                s_ref[d] = state[d] * el_ref[slot_r, idx[d], 0:1, :] + fold
            yield

    phases = [pre_phase(), state_phase()]
    while phases:
        for p in list(phases):
            if next(p, StopIteration) is StopIteration:
                phases.remove(p)


def _gate_expanders():
    e = np.zeros((2, 2 * GATE_PAD, 2 * GROUP_W), np.float32)
    for d in range(2):
        for h in range(N_HEADS):
            for half in range(2):
                r0 = half * GATE_PAD
                e[d, r0 + N_HEADS * d + h, h * HEAD_DIM:(h + 1) * HEAD_DIM] = 1.0
                e[d, r0 + N_GATE_COLS // 2 + N_HEADS * d + h, GROUP_W + h * HEAD_DIM:GROUP_W + (h + 1) * HEAD_DIM] = 1.0
    return jnp.asarray(e, dtype=BF16)


def _block_diag_masks():
    i = np.arange(GROUP_W)
    r, cc = i[:, None], i[None, :]
    same_head = (r >> HEAD_SHIFT) == (cc >> HEAD_SHIFT)
    rl, cl = r & (HEAD_DIM - 1), cc & (HEAD_DIM - 1)
    masks = [same_head]
    sh = 1
    while (2 << sh) <= HEAD_DIM:
        masks.append(same_head & ((rl >> (sh + 1)) == (cl >> (sh + 1))) & ((rl >> sh) != (cl >> sh)))
        sh += 1
    return jnp.asarray(np.stack(masks).astype(np.float32), dtype=BF16)


def _gdn_scan(qkvn, gb, ng):
    b, l, c3 = qkvn.shape
    rows = ng * GDN_CHUNK
    nsteps = l // rows
    grp_in = lambda i: jnp.minimum(i, nsteps - 1)
    grp_out = lambda i: jnp.maximum(i - 1, 0)
    fwd = lambda w, f: pl.BlockSpec((None, rows, w), lambda bi, i: (bi, f(i), 0))
    bwd = lambda w, f: pl.BlockSpec((None, rows, w), lambda bi, i: (bi, nsteps - 1 - f(i), 0))
    ns = 2 * ng
    c = GDN_CHUNK
    masks = _block_diag_masks()
    return pl.pallas_call(
        _gdn_scan_kernel,
        grid=(b, nsteps + 1),
        in_specs=[fwd(c3, grp_in), fwd(GATE_PAD, grp_in), bwd(c3, grp_in), bwd(GATE_PAD, grp_in),
                  pl.BlockSpec((2, 2 * GATE_PAD, 2 * GROUP_W), lambda bi, i: (0, 0, 0)),
                  pl.BlockSpec(masks.shape, lambda bi, i: (0, 0, 0))],
        out_specs=[fwd(GROUP_W, grp_out), bwd(GROUP_W, grp_out)],
        out_shape=[jax.ShapeDtypeStruct((b, l, GROUP_W), BF16)] * 2,
        scratch_shapes=[pltpu.VMEM((2, HEAD_DIM, GROUP_W), F32),
                        pltpu.VMEM((2, ns, c, GROUP_W), F32),
                        pltpu.VMEM((2, ns, 2 * c, GROUP_W), BF16),
                        pltpu.VMEM((2, ns, c, GROUP_W), BF16),
                        pltpu.VMEM((2, ns, GROUP_W, c), BF16),
                        pltpu.VMEM((2, ns, SUBLANES, GROUP_W), F32)],
        compiler_params=_cparams(("parallel", "arbitrary")),
        name="gdn_scan",
    )(qkvn, gb, qkvn, gb, _gate_expanders(), masks)


def _head_stack(q):
    lane_head = lax.broadcasted_iota(jnp.int32, q.shape, 1) >> HEAD_SHIFT
    zero = jnp.zeros((), q.dtype)
    return jnp.concatenate([jnp.where(lane_head == h, q, zero) for h in range(N_HEADS)], axis=0)


def _head_unstack(pv, m):
    lane_head = lax.broadcasted_iota(jnp.int32, (m, GROUP_W), 1) >> HEAD_SHIFT
    out = jnp.where(lane_head == 0, pv[0:m], 0.0)
    for h in range(1, N_HEADS):
        out = out + jnp.where(lane_head == h, pv[h * m:(h + 1) * m], 0.0)
    return out


def _attend(qs, ks, vs, biases):
    s = [_dot_nt(a, b) for a, b in zip(qs, ks)]
    s = [a if b is None else a + b for a, b in zip(s, biases)]
    m = [jnp.max(a, axis=-1, keepdims=True) for a in s]
    p = [jnp.exp(a - b) for a, b in zip(s, m)]
    den = [jnp.sum(a, axis=-1, keepdims=True) for a in p]
    pv = [jnp.dot(a.astype(BF16), b, preferred_element_type=F32) for a, b in zip(p, vs)]
    return [a / b for a, b in zip(pv, den)]


def _natten_kernel(q_ref, kp_ref, kc_ref, kn_ref, vp_ref, vc_ref, vn_ref, bias_ref, o_ref, kwin_ref, vwin_ref):
    gi = pl.program_id(1)
    rows = pl.num_programs(1) * NA_KH
    blk = kc_ref.shape[0]
    for t, (kr, vr) in enumerate(((kp_ref, vp_ref), (kc_ref, vc_ref), (kn_ref, vn_ref))):
        kwin_ref[t * blk:(t + 1) * blk, :] = kr[...]
        vwin_ref[t * blk:(t + 1) * blk, :] = vr[...]

    def body(t, carry):
        js = [t * NA_ROWS_PER_ITER + i for i in range(NA_ROWS_PER_ITER)]
        r = [gi * NA_KH + j for j in js]
        rs = [jnp.clip(a - NA_KH // 2, 0, rows - NA_KH) for a in r]
        start = [pl.multiple_of((a - (gi - 1) * NA_KH) * GRID_W, GRID_W) for a in rs]
        qrows = [pl.ds(pl.multiple_of(j * GRID_W, GRID_W), GRID_W) for j in js]
        qs = [_head_stack(q_ref[a, :] * (HEAD_DIM ** -0.5)) for a in qrows]
        kw = [kwin_ref[pl.ds(a, NA_KH * GRID_W), :] for a in start]
        vw = [vwin_ref[pl.ds(a, NA_KH * GRID_W), :] for a in start]
        bias = [bias_ref[a - b] for a, b in zip(r, rs)]
        for a, pv in zip(qrows, _attend(qs, kw, vw, bias)):
            o_ref[a, :] = _head_unstack(pv, GRID_W).astype(o_ref.dtype)
        return carry

    lax.fori_loop(0, NA_KH // NA_ROWS_PER_ITER, body, 0)


def _natten_bias(rpb):
    w = np.arange(GRID_W)
    cs = np.clip(w - NA_KW // 2, 0, GRID_W - NA_KW)
    wk = np.arange(GRID_W)
    in_win = (wk[None, :] >= cs[:, None]) & (wk[None, :] < cs[:, None] + NA_KW)
    col_off = wk[None, :] - w[:, None] + NA_KW - 1
    delta = np.arange(NA_KH)
    i = np.arange(NA_KH)
    row_off = i[None, :] - delta[:, None] + NA_KH - 1
    sel_r = (row_off[:, :, None] == np.arange(2 * NA_KH - 1)).astype(np.float32)
    sel_c = ((col_off[:, :, None] == np.arange(2 * NA_KW - 1)) & in_win[:, :, None]).astype(np.float32)
    t = jnp.einsum('hab,dia,wvb->hdiwv', rpb.astype(F32), sel_r, sel_c, precision=lax.Precision.HIGHEST)
    t = jnp.where(jnp.asarray(in_win)[None, None, None], t, NEG_BIG)
    t = t.transpose(1, 0, 3, 2, 4)
    return t.reshape(NA_KH, N_HEADS * GRID_W, NA_KH * GRID_W).astype(F32)


def _natten(cqkv, bias):
    b, l, _ = cqkv.shape
    blk = NA_KH * GRID_W
    ng = l // blk
    spec = lambda colblk, f: pl.BlockSpec((None, blk, GROUP_W), lambda bi, g: (bi, f(g), colblk))
    cur = lambda g: g
    prev = lambda g: jnp.maximum(g - 1, 0)
    nxt = lambda g: jnp.minimum(g + 1, ng - 1)
    return pl.pallas_call(
        _natten_kernel,
        grid=(b, ng),
        in_specs=[spec(0, cur), spec(1, prev), spec(1, cur), spec(1, nxt),
                  spec(2, prev), spec(2, cur), spec(2, nxt),
                  pl.BlockSpec(bias.shape, lambda bi, g: (0, 0, 0))],
        out_specs=pl.BlockSpec((None, blk, GROUP_W), lambda bi, g: (bi, g, 0)),
        out_shape=jax.ShapeDtypeStruct((b, l, GROUP_W), BF16),
        scratch_shapes=[pltpu.VMEM((3 * blk, GROUP_W), BF16), pltpu.VMEM((3 * blk, GROUP_W), BF16)],
        compiler_params=_cparams(("parallel", "parallel")),
        name="natten",
    )(cqkv, cqkv, cqkv, cqkv, cqkv, cqkv, cqkv, bias)


def _memattn(q_ref, k_ref, v_ref):
    ts = q_ref.shape[0] // MEM_Q_SPLIT
    parts = [pl.ds(i * ts, ts) for i in range(MEM_Q_SPLIT)]
    qs = [_head_stack(q_ref[a, :] * (HEAD_DIM ** -0.5)) for a in parts]
    k = k_ref[...]
    v = v_ref[...]
    pv = _attend(qs, [k] * MEM_Q_SPLIT, [v] * MEM_Q_SPLIT, [None] * MEM_Q_SPLIT)
    return [_head_unstack(a, ts) for a in pv]


def _outproj_kernel(ya_ref, of_ref, ob_ref, yc_ref, dq_ref, z_ref, x_ref, mem_ref, mg_ref, wkv_ref,
                    gg_ref, pg_ref, hs_ref, w_ref, o_ref, k_ref, v_ref):
    @pl.when(pl.program_id(1) == 0)
    def _():
        m = mem_ref[...]
        ms = jnp.mean(m * m, axis=-1, keepdims=True)
        mn = (m * lax.rsqrt(ms + EPS) * mg_ref[...]).astype(BF16)
        kv = jnp.dot(mn, wkv_ref[...], preferred_element_type=F32)
        k_ref[...] = kv[:, :GROUP_W].astype(BF16)
        v_ref[...] = kv[:, GROUP_W:].astype(BF16)

    yd = jnp.concatenate(_memattn(dq_ref, k_ref, v_ref), axis=0)
    o = of_ref[...].astype(F32) + ob_ref[...].astype(F32)
    ms = _dot(o * o, hs_ref[...]) * (1.0 / HEAD_DIM)
    yb = o * lax.rsqrt(ms + EPS) * gg_ref[...]
    acc = None
    for i, y in enumerate((ya_ref[...], yb, yc_ref[...], yd)):
        gated = (y.astype(F32) * _silu(z_ref[:, i * GROUP_W:(i + 1) * GROUP_W].astype(F32))).astype(BF16)
        part = jnp.dot(gated, w_ref[i * GROUP_W:(i + 1) * GROUP_W, :], preferred_element_type=F32)
        acc = part if acc is None else acc + part
    ms = jnp.mean(acc * acc, axis=-1, keepdims=True)
    o_ref[...] = x_ref[...] + acc * lax.rsqrt(ms + EPS) * pg_ref[...]


def _outproj(ya, o_f, o_b, yc, dq, z, x, mem, mem_g, w_mem_kv, gdn_g, post_g, w_out, tm):
    b, l, d = x.shape
    m = mem.shape[1]
    grp = pl.BlockSpec((None, tm, GROUP_W), lambda bi, i: (bi, i, 0))
    full = pl.BlockSpec((None, tm, d), lambda bi, i: (bi, i, 0))
    const = lambda shape: pl.BlockSpec(shape, lambda bi, i: (0, 0))
    return pl.pallas_call(
        _outproj_kernel,
        grid=(b, l // tm),
        in_specs=[grp, grp, grp, grp, grp, full, full,
                  pl.BlockSpec((None, m, d), lambda bi, i: (bi, 0, 0)), const((1, d)), const((d, 2 * GROUP_W)),
                  const((1, GROUP_W)), const((1, d)), const((GROUP_W, GROUP_W)), const((d, d))],
        out_specs=full,
        out_shape=jax.ShapeDtypeStruct((b, l, d), F32),
        scratch_shapes=[pltpu.VMEM((m, GROUP_W), BF16), pltpu.VMEM((m, GROUP_W), BF16)],
        compiler_params=_cparams(("parallel", "arbitrary")),
        name="outproj",
    )(ya, o_f, o_b, yc, dq, z, x, mem, mem_g.reshape(1, d), w_mem_kv.astype(BF16),
      jnp.tile(gdn_g, N_HEADS).reshape(1, GROUP_W), post_g.reshape(1, d), _head_sum_matrix(), w_out.astype(BF16))


def _tile(l, want):
    t = min(want, l)
    assert l % t == 0
    return t


def _layer(x, mem, tables, pre_g, post_g, w_perm, w_fnet, conv_w, a_log, dt_bias, gdn_g, na_bias, mem_g, w_mem_kv,
           w_out):
    l = x.shape[1]
    u, qkvn, gb, cqkv, dq, z = _inproj(x, pre_g, w_perm, conv_w, a_log, dt_bias, _tile(l, 1024))
    ya = _fnet(u, w_fnet, tables)
    o_f, o_b = _gdn_scan(qkvn, gb, GDN_CHUNKS_PER_STEP)
    yc = _natten(cqkv, na_bias)
    return _outproj(ya, o_f, o_b, yc, dq, z, x, mem, mem_g, w_mem_kv, gdn_g, post_g, w_out, _tile(l, 512))


def kernel(x_prompt, x_sample, mem_prompt, mem_sample, pre_norm_g, post_norm_g, w_in, w_fnet, gdn_conv_w,
           gdn_a_log, gdn_dt_bias, gdn_norm_g, na_rpb, mem_norm_g, w_mem_kv, w_out):
    l = x_prompt.shape[1]
    assert x_prompt.shape[1:] == x_sample.shape[1:] == (l, D_MODEL)
    assert l % (NA_KH * GRID_W) == 0 and l % (GDN_CHUNKS_PER_STEP * GDN_CHUNK) == 0 and l % FFT_NA == 0
    depth = pre_norm_g.shape[0]
    tables = _fnet_tables(l)
    w_perm = [_permute_w_in(w_in[i]) for i in range(depth)]
    na_bias = [_natten_bias(na_rpb[i]) for i in range(depth)]

    def trunk(x, mem):
        for i in range(depth):
            x = _layer(x, mem, tables, pre_norm_g[i], post_norm_g[i], w_perm[i], w_fnet[i], gdn_conv_w[i],
                       gdn_a_log[i], gdn_dt_bias[i], gdn_norm_g[i], na_bias[i], mem_norm_g[i], w_mem_kv[i], w_out[i])
        return x

    return (trunk(x_prompt, mem_prompt), trunk(x_sample, mem_sample))
```

```python
import functools
import math

import numpy as np
import jax
import jax.numpy as jnp
from jax import lax
from jax.experimental import pallas as pl
from jax.experimental.pallas import tpu as pltpu

F32 = jnp.float32
BF16 = jnp.bfloat16

D_MODEL = 1024
GROUP_W = 256
HEAD_DIM = 64
HEAD_SHIFT = 6
N_HEADS = 4
FNET_BLOCK_W = 64
GDN_CHUNK = 64
CONV_K = 5
GRID_W = 64
NA_KH = 8
NA_KW = 16
N_GATE_COLS = 16
EPS = 1e-6
NEG_BIG = -1e30

LANES = 128
SUBLANES = 8
GATE_PAD = LANES
D_PROJ = GROUP_W + 3 * GROUP_W + GATE_PAD + 3 * GROUP_W + GROUP_W + 4 * GROUP_W
FFT_NA = 64
FNET_KA_PER_STEP = 16
GDN_CHUNKS_PER_STEP = 8
NA_ROWS_PER_ITER = 8
MEM_Q_SPLIT = 4
VMEM_LIMIT = 56 * 1024 * 1024


def _cparams(sem):
    return pltpu.CompilerParams(dimension_semantics=sem, vmem_limit_bytes=VMEM_LIMIT)


def _dot(a, b):
    return jnp.dot(a.astype(BF16), b.astype(BF16), preferred_element_type=F32)


def _dot_nt(a, b):
    return lax.dot_general(a.astype(BF16), b.astype(BF16), (((1,), (1,)), ((), ())),
                           preferred_element_type=F32)


def _dot_split(a, b_bf16):
    hi = a.astype(BF16)
    lo = (a - hi.astype(F32)).astype(BF16)
    return (jnp.dot(hi, b_bf16, preferred_element_type=F32)
            + jnp.dot(lo, b_bf16, preferred_element_type=F32))


def _sigmoid(x):
    return 1.0 / (1.0 + jnp.exp(-x))


def _silu(x):
    return x * _sigmoid(x)


def _head_sum_matrix():
    idx = np.arange(GROUP_W) // HEAD_DIM
    return jnp.asarray((idx[:, None] == idx[None, :]).astype(np.float32), dtype=BF16)


def _inproj_kernel(x_ref, xp_ref, xn_ref, g_ref, w_ref, cw_ref, gp_ref, hs_ref, tri_ref,
                   u_ref, qkv_ref, gb_ref, cqkv_ref, dq_ref, z_ref, win_ref):
    i = pl.program_id(1)
    n = pl.num_programs(1)
    tm = x_ref.shape[0]
    halo = xp_ref.shape[0]
    o_bqkv, o_gate, o_cqkv = GROUP_W, 4 * GROUP_W, 4 * GROUP_W + GATE_PAD
    o_dq, o_z = o_cqkv + 3 * GROUP_W, o_cqkv + 4 * GROUP_W

    def normed(x):
        ms = jnp.mean(x * x, axis=-1, keepdims=True)
        return (x * lax.rsqrt(ms + EPS) * g_ref[...]).astype(BF16)

    def proj(hh, lo, hi):
        return jnp.dot(hh, w_ref[:, lo:hi], preferred_element_type=F32)

    h = normed(x_ref[...])
    h_halo = normed(jnp.concatenate([xp_ref[...], xn_ref[...]], axis=0))
    bq = proj(jnp.concatenate([h, h_halo], axis=0), o_bqkv, o_gate)
    win_ref[halo:halo + tm, :] = bq[:tm]
    win_ref[0:halo, :] = bq[tm:tm + halo] * (i > 0).astype(F32)
    win_ref[halo + tm:, :] = bq[tm + halo:] * (i < n - 1).astype(F32)
    a = proj(h, o_gate, o_cqkv)
    pad = CONV_K // 2
    hs = hs_ref[...]
    ts = tri_ref.shape[1]
    lane = lax.broadcasted_iota(jnp.int32, (ts, GATE_PAD), 1)

    def gdn_front(j):
        rows = pl.ds(j * ts, ts)
        acc = cw_ref[0:1, :] * win_ref[pl.ds(halo - pad + j * ts, ts), :]
        for t in range(1, CONV_K):
            acc = acc + cw_ref[t:t + 1, :] * win_ref[pl.ds(halo - pad + t + j * ts, ts), :]
        act = _silu(acc)
        q = act[:, 0:GROUP_W]
        k = act[:, GROUP_W:2 * GROUP_W]
        qkv_ref[rows, 0:GROUP_W] = (q * lax.rsqrt(_dot(q * q, hs) + EPS) * (HEAD_DIM ** -0.5)).astype(qkv_ref.dtype)
        qkv_ref[rows, GROUP_W:2 * GROUP_W] = (k * lax.rsqrt(_dot(k * k, hs) + EPS)).astype(qkv_ref.dtype)
        qkv_ref[rows, 2 * GROUP_W:] = act[:, 2 * GROUP_W:].astype(qkv_ref.dtype)
        aj = a[j * ts:(j + 1) * ts]
        xg = aj + gp_ref[1:2, :]
        softplus = jnp.maximum(xg, 0.0) + jnp.log(1.0 + jnp.exp(-jnp.abs(xg)))
        g = -jnp.exp(gp_ref[0:1, :]) * softplus
        gcf = _dot_split_left(tri_ref[0], g)
        gcb = _dot_split_left(tri_ref[1], g)
        gb_ref[rows, :] = jnp.where(lane < N_HEADS, gcf, jnp.where(lane < 2 * N_HEADS, gcb, _sigmoid(aj)))

    def store(ref, lo, hi, col=None):
        val = proj(h, lo, hi).astype(ref.dtype)
        if col is None:
            ref[...] = val
        else:
            ref[:, col:col + hi - lo] = val

    pieces = [lambda: store(cqkv_ref, o_cqkv, o_dq),
              lambda: (store(dq_ref, o_dq, o_z), store(u_ref, 0, o_bqkv)),
              lambda: store(z_ref, o_z, o_z + 2 * GROUP_W, 0),
              lambda: store(z_ref, o_z + 2 * GROUP_W, o_z + 4 * GROUP_W, 2 * GROUP_W)]
    nsub = tm // ts
    for j in range(max(nsub, len(pieces))):
        if j < len(pieces):
            pieces[j]()
        if j < nsub:
            gdn_front(j)


def _permute_w_in(w_in):
    g = GROUP_W
    o_az, o_bqkv, o_bz, o_bg = g, 2 * g, 5 * g, 6 * g
    o_cqkv = o_bg + N_GATE_COLS
    o_cz, o_dq, o_dz = o_cqkv + 3 * g, o_cqkv + 4 * g, o_cqkv + 5 * g
    pad = jnp.zeros((D_MODEL, GATE_PAD - N_GATE_COLS), w_in.dtype)
    cols = [w_in[:, 0:g], w_in[:, o_bqkv:o_bz], w_in[:, o_bg:o_cqkv], pad, w_in[:, o_cqkv:o_cz],
            w_in[:, o_dq:o_dz], w_in[:, o_az:o_bqkv], w_in[:, o_bz:o_bg], w_in[:, o_cz:o_dq],
            w_in[:, o_dz:o_dz + g]]
    return jnp.concatenate(cols, axis=1).astype(BF16)


def _inproj(x, pre_g, w_perm, conv_w, a_log, dt_bias, tm):
    b, l, d = x.shape
    widths = (GROUP_W, 3 * GROUP_W, GATE_PAD, 3 * GROUP_W, GROUP_W, 4 * GROUP_W)
    dtypes = (BF16, BF16, F32, BF16, BF16, BF16)
    c3 = 3 * GROUP_W
    halo = SUBLANES
    per = tm // halo
    nh = l // halo
    ts = min(tm, 4 * GDN_CHUNK)
    cw = jnp.zeros((SUBLANES, c3), F32).at[:CONV_K].set(conv_w)
    gp = jnp.zeros((SUBLANES, GATE_PAD), F32)
    gp = gp.at[0, :N_GATE_COLS // 2].set(a_log.reshape(-1)).at[1, :N_GATE_COLS // 2].set(dt_bias.reshape(-1))
    ti = np.arange(ts)
    same_chunk = (ti[:, None] // GDN_CHUNK) == (ti[None, :] // GDN_CHUNK)
    tri = np.stack([same_chunk & (ti[None, :] <= ti[:, None]), same_chunk & (ti[None, :] >= ti[:, None])])
    tri = jnp.asarray(tri.astype(np.float32), dtype=BF16)
    row = lambda n: pl.BlockSpec((None, tm, n), lambda bi, i: (bi, i, 0))
    const = lambda shape: pl.BlockSpec(shape, lambda bi, i: (0,) * len(shape))
    return pl.pallas_call(
        _inproj_kernel,
        grid=(b, l // tm),
        in_specs=[row(d),
                  pl.BlockSpec((None, halo, d), lambda bi, i: (bi, jnp.maximum(i * per - 1, 0), 0)),
                  pl.BlockSpec((None, halo, d), lambda bi, i: (bi, jnp.minimum((i + 1) * per, nh - 1), 0)),
                  const((1, d)), const((d, D_PROJ)), const((SUBLANES, c3)), const((SUBLANES, GATE_PAD)),
                  const((GROUP_W, GROUP_W)), const((2, ts, ts))],
        out_specs=[row(n) for n in widths],
        out_shape=[jax.ShapeDtypeStruct((b, l, n), dt) for n, dt in zip(widths, dtypes)],
        scratch_shapes=[pltpu.VMEM((tm + 2 * halo, c3), F32)],
        compiler_params=_cparams(("parallel", "parallel")),
        name="inproj",
    )(x, x, x, pre_g.reshape(1, d), w_perm, cw, gp, _head_sum_matrix(), tri)


def _fnet_a_kernel(x_ref, t_ref, y_ref):
    y_ref[...] = jnp.dot(t_ref[...], x_ref[...].astype(BF16), preferred_element_type=F32).astype(BF16)


def _fnet_b_kernel(y_ref, g_ref, bcs_ref, wf_ref, o_ref):
    nk, nb = o_ref.shape[0], o_ref.shape[1]
    bc = bcs_ref[:GROUP_W, :]
    bs = bcs_ref[GROUP_W:, :]
    wf = wf_ref[...]
    z = [jnp.dot(g_ref[i], y_ref[i], preferred_element_type=F32) for i in range(nk)]
    f = [jnp.dot(a[:nb].astype(BF16), bc, preferred_element_type=F32)
         + jnp.dot(a[nb:].astype(BF16), bs, preferred_element_type=F32) for a in z]
    for i, a in enumerate(f):
        o_ref[i] = jnp.dot(a.astype(BF16), wf, preferred_element_type=F32).astype(o_ref.dtype)


def _fnet_tables(l):
    na, nb = FFT_NA, l // FFT_NA
    ia = np.arange(na)
    ang_a = 2.0 * np.pi * ((ia[:, None] * ia[None, :]) % na) / na
    ta = np.concatenate([np.cos(ang_a), -np.sin(ang_a)], axis=0) / math.sqrt(na)
    ka = jnp.arange(na, dtype=jnp.int32)[:, None, None]
    kb = jnp.arange(nb, dtype=jnp.int32)[None, :, None]
    n2 = jnp.arange(nb, dtype=jnp.int32)[None, None, :]
    ang = (2.0 * math.pi / l) * ((n2 * (ka + na * kb)) % l).astype(F32)
    cg = jnp.cos(ang) / math.sqrt(nb)
    sg = jnp.sin(ang) / math.sqrt(nb)
    gbig = jnp.concatenate([jnp.concatenate([cg, sg], axis=2),
                            jnp.concatenate([-sg, cg], axis=2)], axis=1).astype(BF16)
    ic = np.arange(GROUP_W)
    same = (ic[:, None] // FNET_BLOCK_W) == (ic[None, :] // FNET_BLOCK_W)
    ang_c = 2.0 * np.pi * (((ic[:, None] % FNET_BLOCK_W) * (ic[None, :] % FNET_BLOCK_W)) % FNET_BLOCK_W) / FNET_BLOCK_W
    bc = np.where(same, np.cos(ang_c), 0.0) / math.sqrt(FNET_BLOCK_W)
    bs = np.where(same, np.sin(ang_c), 0.0) / math.sqrt(FNET_BLOCK_W)
    bcs = np.concatenate([bc, bs], axis=0)
    return jnp.asarray(ta, dtype=F32).astype(BF16), gbig, jnp.asarray(bcs, dtype=F32).astype(BF16)


def _fnet(u, w_fnet, tables):
    b, l, c = u.shape
    na, nb = FFT_NA, l // FFT_NA
    ta, gbig, bcs = tables
    cw = min(8192, nb * c)
    x2 = u.reshape(b, na, nb * c)
    y = pl.pallas_call(
        _fnet_a_kernel,
        grid=(b, nb * c // cw),
        in_specs=[pl.BlockSpec((None, na, cw), lambda bi, j: (bi, 0, j)),
                  pl.BlockSpec((2 * na, na), lambda bi, j: (0, 0))],
        out_specs=pl.BlockSpec((None, 2 * na, cw), lambda bi, j: (bi, 0, j)),
        out_shape=jax.ShapeDtypeStruct((b, 2 * na, nb * c), BF16),
        compiler_params=_cparams(("parallel", "parallel")),
        name="fnet_a",
    )(x2, ta)
    ys = y.reshape(b, 2, na, nb, c).transpose(0, 2, 1, 3, 4).reshape(b, na, 2 * nb, c)
    o = pl.pallas_call(
        _fnet_b_kernel,
        grid=(b, na // FNET_KA_PER_STEP),
        in_specs=[pl.BlockSpec((None, FNET_KA_PER_STEP, 2 * nb, c), lambda bi, k: (bi, k, 0, 0)),
                  pl.BlockSpec((FNET_KA_PER_STEP, 2 * nb, 2 * nb), lambda bi, k: (k, 0, 0)),
                  pl.BlockSpec((2 * c, c), lambda bi, k: (0, 0)),
                  pl.BlockSpec((c, c), lambda bi, k: (0, 0))],
        out_specs=pl.BlockSpec((None, FNET_KA_PER_STEP, nb, c), lambda bi, k: (bi, k, 0, 0)),
        out_shape=jax.ShapeDtypeStruct((b, na, nb, c), BF16),
        compiler_params=_cparams(("parallel", "parallel")),
        name="fnet_b",
    )(ys, gbig, bcs, w_fnet.astype(BF16))
    return o.transpose(0, 2, 1, 3).reshape(b, l, c)


def _dot_split_left(m_bf16, a):
    hi = a.astype(BF16)
    lo = (a - hi.astype(F32)).astype(BF16)
    return (jnp.dot(m_bf16, hi, preferred_element_type=F32)
            + jnp.dot(m_bf16, lo, preferred_element_type=F32))


def _block_diag(x, mask01):
    xb = x.astype(BF16)
    return jnp.concatenate([xb] * N_HEADS, axis=0) * mask01


def _gdn_scan_kernel(xf_ref, gbf_ref, xb_ref, gbb_ref, ex_ref, bdm_ref, of_ref, ob_ref,
                     s_ref, u_ref, wq_ref, qk_ref, kdt_ref, el_ref):
    c = GDN_CHUNK
    ng = xf_ref.shape[0] // c
    step = pl.program_id(1)
    slot_w = step % 2
    slot_r = 1 - slot_w

    @pl.when(step == 0)
    def _():
        for ref in (s_ref, u_ref, wq_ref, qk_ref, kdt_ref, el_ref):
            ref[...] = jnp.zeros_like(ref)

    row = lax.broadcasted_iota(jnp.int32, (c, GROUP_W), 0)
    lane = lax.broadcasted_iota(jnp.int32, (c, GROUP_W), 1)
    colm = lane & (HEAD_DIM - 1)
    lane_head = lane >> HEAD_SHIFT
    incl = [(row >= colm), (row <= colm)]
    strict = [(row > colm), (row < colm)]
    eye_f = (row == colm).astype(F32)
    bd = lambda a: _block_diag(a, bdm_ref[0])

    streams = [(g, d) for g in range(ng) for d in range(2)]

    def rows_of(g, d):
        lc = g if d == 0 else ng - 1 - g
        return pl.ds(lc * c, c)

    x_refs = (xf_ref, xb_ref)
    gb_refs = (gbf_ref, gbb_ref)
    o_refs = (of_ref, ob_ref)

    def pre_phase():
        gb = [gb_refs[d][rows_of(g, d), :] for g, d in streams]
        gb_hi = [a.astype(BF16) for a in gb]
        gb_hl = [jnp.concatenate([hi, (a - hi.astype(F32)).astype(BF16)], axis=1) for a, hi in zip(gb, gb_hi)]
        ex_d = [jnp.dot(jnp.concatenate(gb_hl[d::2], axis=0), ex_ref[d], preferred_element_type=F32)
                for d in range(2)]
        ex = [ex_d[d][g * c:(g + 1) * c] for g, d in streams]
        gc = [a[:, :GROUP_W] for a in ex]
        beta = [a[:, GROUP_W:] for a in ex]
        yield
        gbt = [a.T for a in gb]
        gcr = [jnp.concatenate([t[N_HEADS * d + h:N_HEADS * d + h + 1, :] for h in range(N_HEADS)], axis=1)
               for (g, d), t in zip(streams, gbt)]
        decay = [jnp.exp(jnp.where(incl[d], a - r, -jnp.inf)) for (g, d), a, r in zip(streams, gc, gcr)]
        last = [c - 1, 0]
        g_last = [a[last[d]:last[d] + 1, :] for (g, d), a in zip(streams, gc)]
        e_g = [jnp.exp(a) for a in gc]
        e_rem = [jnp.exp(gl - a) for gl, a in zip(g_last, gc)]
        for i, gl in enumerate(g_last):
            el_ref[slot_w, i] = jnp.broadcast_to(jnp.exp(gl), (SUBLANES, GROUP_W))
        yield
        q = [x_refs[d][rows_of(g, d), 0:GROUP_W] for g, d in streams]
        k = [x_refs[d][rows_of(g, d), GROUP_W:2 * GROUP_W] for g, d in streams]
        v = [x_refs[d][rows_of(g, d), 2 * GROUP_W:3 * GROUP_W] for g, d in streams]
        kb = [a * b for a, b in zip(k, beta)]
        kq = [_dot_nt(jnp.concatenate([a, b], axis=0), bd(kk)) for a, b, kk in zip(kb, q, k)]
        lm = [jnp.where(strict[d], a[:c] * dec, 0.0) for (g, d), a, dec in zip(streams, kq, decay)]
        yield
        for i, (a, dec) in enumerate(zip(kq, decay)):
            qk_ref[slot_w, i] = (a[c:] * dec).astype(BF16)
        for i, (a, e) in enumerate(zip(k, e_rem)):
            kdt_ref[slot_w, i] = (a * e).T.astype(BF16)
        minv = [eye_f - jnp.where((row >> 1) == (colm >> 1), a, 0.0) for a in lm]
        lm_b = [a.astype(BF16) for a in lm]
        sh = 1
        while (2 << sh) <= c:
            t = [_dot(m, _block_diag(a, bdm_ref[sh])) for m, a in zip(minv, lm_b)]
            yield
            t = [_dot(a, bd(m)) for a, m in zip(t, minv)]
            minv = [m - a for m, a in zip(minv, t)]
            yield
            sh += 1
        for i, (m, a, b) in enumerate(zip(minv, v, beta)):
            u_ref[slot_w, i] = _dot(m, bd(a * b))
        yield
        for i, (m, a, e, qq) in enumerate(zip(minv, kb, e_g, q)):
            wq_ref[slot_w, i] = jnp.concatenate([_dot(m, bd(a * e)), qq * e], axis=0).astype(BF16)

    def state_phase():
        for g in range(ng):
            idx = [2 * g, 2 * g + 1]
            state = [s_ref[d] for d in range(2)]
            ws = [jnp.dot(wq_ref[slot_r, i], bd(s), preferred_element_type=F32) for i, s in zip(idx, state)]
            yield
            v_new = [u_ref[slot_r, i] - a[:c] for i, a in zip(idx, ws)]
            o2 = [jnp.dot(qk_ref[slot_r, i], bd(a), preferred_element_type=F32) for i, a in zip(idx, v_new)]
            upd = [jnp.dot(kdt_ref[slot_r, i], a.astype(BF16), preferred_element_type=F32)
                   for i, a in zip(idx, v_new)]
            for d in range(2):
                o_refs[d][rows_of(g, d), :] = (ws[d][c:] + o2[d]).astype(o_refs[d].dtype)
                fold = jnp.where(lane_head == 0, upd[d][0:c], 0.0)
                for h in range(1, N_HEADS):
                    fold = fold + jnp.where(lane_head == h, upd[d][h * c:(h + 1) * c], 0.0)
                s_ref[d] = state[d] * el_ref[slot_r, idx[d], 0:1, :] + fold
            yield

    phases = [pre_phase(), state_phase()]
    while phases:
        for p in list(phases):
            if next(p, StopIteration) is StopIteration:
                phases.remove(p)


def _gate_expanders():
    e = np.zeros((2, 2 * GATE_PAD, 2 * GROUP_W), np.float32)
    for d in range(2):
        for h in range(N_HEADS):
            for half in range(2):
                r0 = half * GATE_PAD
                e[d, r0 + N_HEADS * d + h, h * HEAD_DIM:(h + 1) * HEAD_DIM] = 1.0
                e[d, r0 + N_GATE_COLS // 2 + N_HEADS * d + h, GROUP_W + h * HEAD_DIM:GROUP_W + (h + 1) * HEAD_DIM] = 1.0
    return jnp.asarray(e, dtype=BF16)


def _block_diag_masks():
    i = np.arange(GROUP_W)
    r, cc = i[:, None], i[None, :]
    same_head = (r >> HEAD_SHIFT) == (cc >> HEAD_SHIFT)
    rl, cl = r & (HEAD_DIM - 1), cc & (HEAD_DIM - 1)
    masks = [same_head]
    sh = 1
    while (2 << sh) <= HEAD_DIM:
        masks.append(same_head & ((rl >> (sh + 1)) == (cl >> (sh + 1))) & ((rl >> sh) != (cl >> sh)))
        sh += 1
    return jnp.asarray(np.stack(masks).astype(np.float32), dtype=BF16)


def _gdn_scan(qkvn, gb, ng):
    b, l, c3 = qkvn.shape
    rows = ng * GDN_CHUNK
    nsteps = l // rows
    grp_in = lambda i: jnp.minimum(i, nsteps - 1)
    grp_out = lambda i: jnp.maximum(i - 1, 0)
    fwd = lambda w, f: pl.BlockSpec((None, rows, w), lambda bi, i: (bi, f(i), 0))
    bwd = lambda w, f: pl.BlockSpec((None, rows, w), lambda bi, i: (bi, nsteps - 1 - f(i), 0))
    ns = 2 * ng
    c = GDN_CHUNK
    masks = _block_diag_masks()
    return pl.pallas_call(
        _gdn_scan_kernel,
        grid=(b, nsteps + 1),
        in_specs=[fwd(c3, grp_in), fwd(GATE_PAD, grp_in), bwd(c3, grp_in), bwd(GATE_PAD, grp_in),
                  pl.BlockSpec((2, 2 * GATE_PAD, 2 * GROUP_W), lambda bi, i: (0, 0, 0)),
                  pl.BlockSpec(masks.shape, lambda bi, i: (0, 0, 0))],
        out_specs=[fwd(GROUP_W, grp_out), bwd(GROUP_W, grp_out)],
        out_shape=[jax.ShapeDtypeStruct((b, l, GROUP_W), BF16)] * 2,
        scratch_shapes=[pltpu.VMEM((2, HEAD_DIM, GROUP_W), F32),
                        pltpu.VMEM((2, ns, c, GROUP_W), F32),
                        pltpu.VMEM((2, ns, 2 * c, GROUP_W), BF16),
                        pltpu.VMEM((2, ns, c, GROUP_W), BF16),
                        pltpu.VMEM((2, ns, GROUP_W, c), BF16),
                        pltpu.VMEM((2, ns, SUBLANES, GROUP_W), F32)],
        compiler_params=_cparams(("parallel", "arbitrary")),
        name="gdn_scan",
    )(qkvn, gb, qkvn, gb, _gate_expanders(), masks)


def _head_stack(q):
    lane_head = lax.broadcasted_iota(jnp.int32, q.shape, 1) >> HEAD_SHIFT
    zero = jnp.zeros((), q.dtype)
    return jnp.concatenate([jnp.where(lane_head == h, q, zero) for h in range(N_HEADS)], axis=0)


def _head_unstack(pv, m):
    lane_head = lax.broadcasted_iota(jnp.int32, (m, GROUP_W), 1) >> HEAD_SHIFT
    out = jnp.where(lane_head == 0, pv[0:m], 0.0)
    for h in range(1, N_HEADS):
        out = out + jnp.where(lane_head == h, pv[h * m:(h + 1) * m], 0.0)
    return out


def _attend(qs, ks, vs, biases):
    s = [_dot_nt(a, b) for a, b in zip(qs, ks)]
    s = [a if b is None else a + b for a, b in zip(s, biases)]
    m = [jnp.max(a, axis=-1, keepdims=True) for a in s]
    p = [jnp.exp(a - b) for a, b in zip(s, m)]
    den = [jnp.sum(a, axis=-1, keepdims=True) for a in p]
    pv = [jnp.dot(a.astype(BF16), b, preferred_element_type=F32) for a, b in zip(p, vs)]
    return [a / b for a, b in zip(pv, den)]


def _natten_kernel(q_ref, kp_ref, kc_ref, kn_ref, vp_ref, vc_ref, vn_ref, bias_ref, o_ref, kwin_ref, vwin_ref):
    gi = pl.program_id(1)
    rows = pl.num_programs(1) * NA_KH
    blk = kc_ref.shape[0]
    for t, (kr, vr) in enumerate(((kp_ref, vp_ref), (kc_ref, vc_ref), (kn_ref, vn_ref))):
        kwin_ref[t * blk:(t + 1) * blk, :] = kr[...]
        vwin_ref[t * blk:(t + 1) * blk, :] = vr[...]

    def body(t, carry):
        js = [t * NA_ROWS_PER_ITER + i for i in range(NA_ROWS_PER_ITER)]
        r = [gi * NA_KH + j for j in js]
        rs = [jnp.clip(a - NA_KH // 2, 0, rows - NA_KH) for a in r]
        start = [pl.multiple_of((a - (gi - 1) * NA_KH) * GRID_W, GRID_W) for a in rs]
        qrows = [pl.ds(pl.multiple_of(j * GRID_W, GRID_W), GRID_W) for j in js]
        qs = [_head_stack(q_ref[a, :] * (HEAD_DIM ** -0.5)) for a in qrows]
        kw = [kwin_ref[pl.ds(a, NA_KH * GRID_W), :] for a in start]
        vw = [vwin_ref[pl.ds(a, NA_KH * GRID_W), :] for a in start]
        bias = [bias_ref[a - b] for a, b in zip(r, rs)]
        for a, pv in zip(qrows, _attend(qs, kw, vw, bias)):
            o_ref[a, :] = _head_unstack(pv, GRID_W).astype(o_ref.dtype)
        return carry

    lax.fori_loop(0, NA_KH // NA_ROWS_PER_ITER, body, 0)


def _natten_bias(rpb):
    w = np.arange(GRID_W)
    cs = np.clip(w - NA_KW // 2, 0, GRID_W - NA_KW)
    wk = np.arange(GRID_W)
    in_win = (wk[None, :] >= cs[:, None]) & (wk[None, :] < cs[:, None] + NA_KW)
    col_off = wk[None, :] - w[:, None] + NA_KW - 1
    delta = np.arange(NA_KH)
    i = np.arange(NA_KH)
    row_off = i[None, :] - delta[:, None] + NA_KH - 1
    sel_r = (row_off[:, :, None] == np.arange(2 * NA_KH - 1)).astype(np.float32)
    sel_c = ((col_off[:, :, None] == np.arange(2 * NA_KW - 1)) & in_win[:, :, None]).astype(np.float32)
    t = jnp.einsum('hab,dia,wvb->hdiwv', rpb.astype(F32), sel_r, sel_c, precision=lax.Precision.HIGHEST)
    t = jnp.where(jnp.asarray(in_win)[None, None, None], t, NEG_BIG)
    t = t.transpose(1, 0, 3, 2, 4)
    return t.reshape(NA_KH, N_HEADS * GRID_W, NA_KH * GRID_W).astype(F32)


def _natten(cqkv, bias):
    b, l, _ = cqkv.shape
    blk = NA_KH * GRID_W
    ng = l // blk
    spec = lambda colblk, f: pl.BlockSpec((None, blk, GROUP_W), lambda bi, g: (bi, f(g), colblk))
    cur = lambda g: g
    prev = lambda g: jnp.maximum(g - 1, 0)
    nxt = lambda g: jnp.minimum(g + 1, ng - 1)
    return pl.pallas_call(
        _natten_kernel,
        grid=(b, ng),
        in_specs=[spec(0, cur), spec(1, prev), spec(1, cur), spec(1, nxt),
                  spec(2, prev), spec(2, cur), spec(2, nxt),
                  pl.BlockSpec(bias.shape, lambda bi, g: (0, 0, 0))],
        out_specs=pl.BlockSpec((None, blk, GROUP_W), lambda bi, g: (bi, g, 0)),
        out_shape=jax.ShapeDtypeStruct((b, l, GROUP_W), BF16),
        scratch_shapes=[pltpu.VMEM((3 * blk, GROUP_W), BF16), pltpu.VMEM((3 * blk, GROUP_W), BF16)],
        compiler_params=_cparams(("parallel", "parallel")),
        name="natten",
    )(cqkv, cqkv, cqkv, cqkv, cqkv, cqkv, cqkv, bias)


def _memattn(q_ref, k_ref, v_ref):
    ts = q_ref.shape[0] // MEM_Q_SPLIT
    parts = [pl.ds(i * ts, ts) for i in range(MEM_Q_SPLIT)]
    qs = [_head_stack(q_ref[a, :] * (HEAD_DIM ** -0.5)) for a in parts]
    k = k_ref[...]
    v = v_ref[...]
    pv = _attend(qs, [k] * MEM_Q_SPLIT, [v] * MEM_Q_SPLIT, [None] * MEM_Q_SPLIT)
    return [_head_unstack(a, ts) for a in pv]


def _outproj_kernel(ya_ref, of_ref, ob_ref, yc_ref, dq_ref, z_ref, x_ref, mem_ref, mg_ref, wkv_ref,
                    gg_ref, pg_ref, hs_ref, w_ref, o_ref, k_ref, v_ref):
    @pl.when(pl.program_id(1) == 0)
    def _():
        m = mem_ref[...]
        ms = jnp.mean(m * m, axis=-1, keepdims=True)
        mn = (m * lax.rsqrt(ms + EPS) * mg_ref[...]).astype(BF16)
        kv = jnp.dot(mn, wkv_ref[...], preferred_element_type=F32)
        k_ref[...] = kv[:, :GROUP_W].astype(BF16)
        v_ref[...] = kv[:, GROUP_W:].astype(BF16)

    yd = jnp.concatenate(_memattn(dq_ref, k_ref, v_ref), axis=0)
    o = of_ref[...].astype(F32) + ob_ref[...].astype(F32)
    ms = _dot(o * o, hs_ref[...]) * (1.0 / HEAD_DIM)
    yb = o * lax.rsqrt(ms + EPS) * gg_ref[...]
    acc = None
    for i, y in enumerate((ya_ref[...], yb, yc_ref[...], yd)):
        gated = (y.astype(F32) * _silu(z_ref[:, i * GROUP_W:(i + 1) * GROUP_W].astype(F32))).astype(BF16)
        part = jnp.dot(gated, w_ref[i * GROUP_W:(i + 1) * GROUP_W, :], preferred_element_type=F32)
        acc = part if acc is None else acc + part
    ms = jnp.mean(acc * acc, axis=-1, keepdims=True)
    o_ref[...] = x_ref[...] + acc * lax.rsqrt(ms + EPS) * pg_ref[...]


def _outproj(ya, o_f, o_b, yc, dq, z, x, mem, mem_g, w_mem_kv, gdn_g, post_g, w_out, tm):
    b, l, d = x.shape
    m = mem.shape[1]
    grp = pl.BlockSpec((None, tm, GROUP_W), lambda bi, i: (bi, i, 0))
    full = pl.BlockSpec((None, tm, d), lambda bi, i: (bi, i, 0))
    const = lambda shape: pl.BlockSpec(shape, lambda bi, i: (0, 0))
    return pl.pallas_call(
        _outproj_kernel,
        grid=(b, l // tm),
        in_specs=[grp, grp, grp, grp, grp, full, full,
                  pl.BlockSpec((None, m, d), lambda bi, i: (bi, 0, 0)), const((1, d)), const((d, 2 * GROUP_W)),
                  const((1, GROUP_W)), const((1, d)), const((GROUP_W, GROUP_W)), const((d, d))],
        out_specs=full,
        out_shape=jax.ShapeDtypeStruct((b, l, d), F32),
        scratch_shapes=[pltpu.VMEM((m, GROUP_W), BF16), pltpu.VMEM((m, GROUP_W), BF16)],
        compiler_params=_cparams(("parallel", "arbitrary")),
        name="outproj",
    )(ya, o_f, o_b, yc, dq, z, x, mem, mem_g.reshape(1, d), w_mem_kv.astype(BF16),
      jnp.tile(gdn_g, N_HEADS).reshape(1, GROUP_W), post_g.reshape(1, d), _head_sum_matrix(), w_out.astype(BF16))


def _tile(l, want):
    t = min(want, l)
    assert l % t == 0
    return t


def _layer(x, mem, tables, pre_g, post_g, w_perm, w_fnet, conv_w, a_log, dt_bias, gdn_g, na_bias, mem_g, w_mem_kv,
           w_out):
    l = x.shape[1]
    u, qkvn, gb, cqkv, dq, z = _inproj(x, pre_g, w_perm, conv_w, a_log, dt_bias, _tile(l, 1024))
    ya = _fnet(u, w_fnet, tables)
    o_f, o_b = _gdn_scan(qkvn, gb, GDN_CHUNKS_PER_STEP)
    yc = _natten(cqkv, na_bias)
    return _outproj(ya, o_f, o_b, yc, dq, z, x, mem, mem_g, w_mem_kv, gdn_g, post_g, w_out, _tile(l, 1024))


def kernel(x_prompt, x_sample, mem_prompt, mem_sample, pre_norm_g, post_norm_g, w_in, w_fnet, gdn_conv_w,
           gdn_a_log, gdn_dt_bias, gdn_norm_g, na_rpb, mem_norm_g, w_mem_kv, w_out):
    l = x_prompt.shape[1]
    assert x_prompt.shape[1:] == x_sample.shape[1:] == (l, D_MODEL)
    assert l % (NA_KH * GRID_W) == 0 and l % (GDN_CHUNKS_PER_STEP * GDN_CHUNK) == 0 and l % FFT_NA == 0
    depth = pre_norm_g.shape[0]
    tables = _fnet_tables(l)
    w_perm = [_permute_w_in(w_in[i]) for i in range(depth)]
    na_bias = [_natten_bias(na_rpb[i]) for i in range(depth)]

    def trunk(x, mem):
        for i in range(depth):
            x = _layer(x, mem, tables, pre_norm_g[i], post_norm_g[i], w_perm[i], w_fnet[i], gdn_conv_w[i],
                       gdn_a_log[i], gdn_dt_bias[i], gdn_norm_g[i], na_bias[i], mem_norm_g[i], w_mem_kv[i], w_out[i])
        return x

    return (trunk(x_prompt, mem_prompt), trunk(x_sample, mem_sample))
```

```python
import functools
import math

import numpy as np
import jax
import jax.numpy as jnp
from jax import lax
from jax.experimental import pallas as pl
from jax.experimental.pallas import tpu as pltpu

F32 = jnp.float32
BF16 = jnp.bfloat16

D_MODEL = 1024
GROUP_W = 256
HEAD_DIM = 64
HEAD_SHIFT = 6
N_HEADS = 4
FNET_BLOCK_W = 64
GDN_CHUNK = 64
CONV_K = 5
GRID_W = 64
NA_KH = 8
NA_KW = 16
N_GATE_COLS = 16
EPS = 1e-6
NEG_BIG = -1e30

LANES = 128
SUBLANES = 8
GATE_PAD = LANES
D_PROJ = GROUP_W + 3 * GROUP_W + GATE_PAD + 3 * GROUP_W + GROUP_W + 4 * GROUP_W
FFT_NA = 64
FNET_KA_PER_STEP = 16
GDN_CHUNKS_PER_STEP = 4
NA_ROWS_PER_ITER = 8
MEM_Q_SPLIT = 4
VMEM_LIMIT = 56 * 1024 * 1024


def _cparams(sem):
    return pltpu.CompilerParams(dimension_semantics=sem, vmem_limit_bytes=VMEM_LIMIT)


def _dot(a, b):
    return jnp.dot(a.astype(BF16), b.astype(BF16), preferred_element_type=F32)


def _dot_nt(a, b):
    return lax.dot_general(a.astype(BF16), b.astype(BF16), (((1,), (1,)), ((), ())),
                           preferred_element_type=F32)


def _dot_split(a, b_bf16):
    hi = a.astype(BF16)
    lo = (a - hi.astype(F32)).astype(BF16)
    return (jnp.dot(hi, b_bf16, preferred_element_type=F32)
            + jnp.dot(lo, b_bf16, preferred_element_type=F32))


def _sigmoid(x):
    return 1.0 / (1.0 + jnp.exp(-x))


def _silu(x):
    return x * _sigmoid(x)


def _head_sum_matrix():
    idx = np.arange(GROUP_W) // HEAD_DIM
    return jnp.asarray((idx[:, None] == idx[None, :]).astype(np.float32), dtype=BF16)


def _inproj_kernel(x_ref, xp_ref, xn_ref, g_ref, w_ref, cw_ref, gp_ref, hs_ref, tri_ref,
                   u_ref, qkv_ref, gb_ref, cqkv_ref, dq_ref, z_ref, win_ref):
    i = pl.program_id(1)
    n = pl.num_programs(1)
    tm = x_ref.shape[0]
    halo = xp_ref.shape[0]
    o_bqkv, o_gate, o_cqkv = GROUP_W, 4 * GROUP_W, 4 * GROUP_W + GATE_PAD
    o_dq, o_z = o_cqkv + 3 * GROUP_W, o_cqkv + 4 * GROUP_W

    def normed(x):
        ms = jnp.mean(x * x, axis=-1, keepdims=True)
        return (x * lax.rsqrt(ms + EPS) * g_ref[...]).astype(BF16)

    def proj(hh, lo, hi):
        return jnp.dot(hh, w_ref[:, lo:hi], preferred_element_type=F32)

    h = normed(x_ref[...])
    h_halo = normed(jnp.concatenate([xp_ref[...], xn_ref[...]], axis=0))
    bq = proj(jnp.concatenate([h, h_halo], axis=0), o_bqkv, o_gate)
    win_ref[halo:halo + tm, :] = bq[:tm]
    win_ref[0:halo, :] = bq[tm:tm + halo] * (i > 0).astype(F32)
    win_ref[halo + tm:, :] = bq[tm + halo:] * (i < n - 1).astype(F32)
    a = proj(h, o_gate, o_cqkv)
    pad = CONV_K // 2
    hs = hs_ref[...]
    ts = tri_ref.shape[1]
    lane = lax.broadcasted_iota(jnp.int32, (ts, GATE_PAD), 1)

    def gdn_front(j):
        rows = pl.ds(j * ts, ts)
        acc = cw_ref[0:1, :] * win_ref[pl.ds(halo - pad + j * ts, ts), :]
        for t in range(1, CONV_K):
            acc = acc + cw_ref[t:t + 1, :] * win_ref[pl.ds(halo - pad + t + j * ts, ts), :]
        act = _silu(acc)
        q = act[:, 0:GROUP_W]
        k = act[:, GROUP_W:2 * GROUP_W]
        qkv_ref[rows, 0:GROUP_W] = (q * lax.rsqrt(_dot(q * q, hs) + EPS) * (HEAD_DIM ** -0.5)).astype(qkv_ref.dtype)
        qkv_ref[rows, GROUP_W:2 * GROUP_W] = (k * lax.rsqrt(_dot(k * k, hs) + EPS)).astype(qkv_ref.dtype)
        qkv_ref[rows, 2 * GROUP_W:] = act[:, 2 * GROUP_W:].astype(qkv_ref.dtype)
        aj = a[j * ts:(j + 1) * ts]
        xg = aj + gp_ref[1:2, :]
        softplus = jnp.maximum(xg, 0.0) + jnp.log(1.0 + jnp.exp(-jnp.abs(xg)))
        g = -jnp.exp(gp_ref[0:1, :]) * softplus
        gcf = _dot_split_left(tri_ref[0], g)
        gcb = _dot_split_left(tri_ref[1], g)
        gb_ref[rows, :] = jnp.where(lane < N_HEADS, gcf, jnp.where(lane < 2 * N_HEADS, gcb, _sigmoid(aj)))

    def store(ref, lo, hi, col=None):
        val = proj(h, lo, hi).astype(ref.dtype)
        if col is None:
            ref[...] = val
        else:
            ref[:, col:col + hi - lo] = val

    pieces = [lambda: store(cqkv_ref, o_cqkv, o_dq),
              lambda: (store(dq_ref, o_dq, o_z), store(u_ref, 0, o_bqkv)),
              lambda: store(z_ref, o_z, o_z + 2 * GROUP_W, 0),
              lambda: store(z_ref, o_z + 2 * GROUP_W, o_z + 4 * GROUP_W, 2 * GROUP_W)]
    nsub = tm // ts
    for j in range(max(nsub, len(pieces))):
        if j < len(pieces):
            pieces[j]()
        if j < nsub:
            gdn_front(j)


def _permute_w_in(w_in):
    g = GROUP_W
    o_az, o_bqkv, o_bz, o_bg = g, 2 * g, 5 * g, 6 * g
    o_cqkv = o_bg + N_GATE_COLS
    o_cz, o_dq, o_dz = o_cqkv + 3 * g, o_cqkv + 4 * g, o_cqkv + 5 * g
    pad = jnp.zeros((D_MODEL, GATE_PAD - N_GATE_COLS), w_in.dtype)
    cols = [w_in[:, 0:g], w_in[:, o_bqkv:o_bz], w_in[:, o_bg:o_cqkv], pad, w_in[:, o_cqkv:o_cz],
            w_in[:, o_dq:o_dz], w_in[:, o_az:o_bqkv], w_in[:, o_bz:o_bg], w_in[:, o_cz:o_dq],
            w_in[:, o_dz:o_dz + g]]
    return jnp.concatenate(cols, axis=1).astype(BF16)


def _inproj(x, pre_g, w_perm, conv_w, a_log, dt_bias, tm):
    b, l, d = x.shape
    widths = (GROUP_W, 3 * GROUP_W, GATE_PAD, 3 * GROUP_W, GROUP_W, 4 * GROUP_W)
    dtypes = (BF16, BF16, F32, BF16, BF16, BF16)
    c3 = 3 * GROUP_W
    halo = SUBLANES
    per = tm // halo
    nh = l // halo
    ts = min(tm, 4 * GDN_CHUNK)
    cw = jnp.zeros((SUBLANES, c3), F32).at[:CONV_K].set(conv_w)
    gp = jnp.zeros((SUBLANES, GATE_PAD), F32)
    gp = gp.at[0, :N_GATE_COLS // 2].set(a_log.reshape(-1)).at[1, :N_GATE_COLS // 2].set(dt_bias.reshape(-1))
    ti = np.arange(ts)
    same_chunk = (ti[:, None] // GDN_CHUNK) == (ti[None, :] // GDN_CHUNK)
    tri = np.stack([same_chunk & (ti[None, :] <= ti[:, None]), same_chunk & (ti[None, :] >= ti[:, None])])
    tri = jnp.asarray(tri.astype(np.float32), dtype=BF16)
    row = lambda n: pl.BlockSpec((None, tm, n), lambda bi, i: (bi, i, 0))
    const = lambda shape: pl.BlockSpec(shape, lambda bi, i: (0,) * len(shape))
    return pl.pallas_call(
        _inproj_kernel,
        grid=(b, l // tm),
        in_specs=[row(d),
                  pl.BlockSpec((None, halo, d), lambda bi, i: (bi, jnp.maximum(i * per - 1, 0), 0)),
                  pl.BlockSpec((None, halo, d), lambda bi, i: (bi, jnp.minimum((i + 1) * per, nh - 1), 0)),
                  const((1, d)), const((d, D_PROJ)), const((SUBLANES, c3)), const((SUBLANES, GATE_PAD)),
                  const((GROUP_W, GROUP_W)), const((2, ts, ts))],
        out_specs=[row(n) for n in widths],
        out_shape=[jax.ShapeDtypeStruct((b, l, n), dt) for n, dt in zip(widths, dtypes)],
        scratch_shapes=[pltpu.VMEM((tm + 2 * halo, c3), F32)],
        compiler_params=_cparams(("parallel", "parallel")),
        name="inproj",
    )(x, x, x, pre_g.reshape(1, d), w_perm, cw, gp, _head_sum_matrix(), tri)


def _fnet_a_kernel(x_ref, t_ref, y_ref):
    y_ref[...] = jnp.dot(t_ref[...], x_ref[...].astype(BF16), preferred_element_type=F32).astype(BF16)


def _fnet_b_kernel(y_ref, g_ref, bcs_ref, wf_ref, o_ref):
    nk, nb = o_ref.shape[0], o_ref.shape[1]
    bc = bcs_ref[:GROUP_W, :]
    bs = bcs_ref[GROUP_W:, :]
    wf = wf_ref[...]
    z = [jnp.dot(g_ref[i], y_ref[i], preferred_element_type=F32) for i in range(nk)]
    f = [jnp.dot(a[:nb].astype(BF16), bc, preferred_element_type=F32)
         + jnp.dot(a[nb:].astype(BF16), bs, preferred_element_type=F32) for a in z]
    for i, a in enumerate(f):
        o_ref[i] = jnp.dot(a.astype(BF16), wf, preferred_element_type=F32).astype(o_ref.dtype)


def _fnet_tables(l):
    na, nb = FFT_NA, l // FFT_NA
    ia = np.arange(na)
    ang_a = 2.0 * np.pi * ((ia[:, None] * ia[None, :]) % na) / na
    ta = np.concatenate([np.cos(ang_a), -np.sin(ang_a)], axis=0) / math.sqrt(na)
    ka = jnp.arange(na, dtype=jnp.int32)[:, None, None]
    kb = jnp.arange(nb, dtype=jnp.int32)[None, :, None]
    n2 = jnp.arange(nb, dtype=jnp.int32)[None, None, :]
    ang = (2.0 * math.pi / l) * ((n2 * (ka + na * kb)) % l).astype(F32)
    cg = jnp.cos(ang) / math.sqrt(nb)
    sg = jnp.sin(ang) / math.sqrt(nb)
    gbig = jnp.concatenate([jnp.concatenate([cg, sg], axis=2),
                            jnp.concatenate([-sg, cg], axis=2)], axis=1).astype(BF16)
    ic = np.arange(GROUP_W)
    same = (ic[:, None] // FNET_BLOCK_W) == (ic[None, :] // FNET_BLOCK_W)
    ang_c = 2.0 * np.pi * (((ic[:, None] % FNET_BLOCK_W) * (ic[None, :] % FNET_BLOCK_W)) % FNET_BLOCK_W) / FNET_BLOCK_W
    bc = np.where(same, np.cos(ang_c), 0.0) / math.sqrt(FNET_BLOCK_W)
    bs = np.where(same, np.sin(ang_c), 0.0) / math.sqrt(FNET_BLOCK_W)
    bcs = np.concatenate([bc, bs], axis=0)
    return jnp.asarray(ta, dtype=F32).astype(BF16), gbig, jnp.asarray(bcs, dtype=F32).astype(BF16)


def _fnet(u, w_fnet, tables):
    b, l, c = u.shape
    na, nb = FFT_NA, l // FFT_NA
    ta, gbig, bcs = tables
    cw = min(8192, nb * c)
    x2 = u.reshape(b, na, nb * c)
    y = pl.pallas_call(
        _fnet_a_kernel,
        grid=(b, nb * c // cw),
        in_specs=[pl.BlockSpec((None, na, cw), lambda bi, j: (bi, 0, j)),
                  pl.BlockSpec((2 * na, na), lambda bi, j: (0, 0))],
        out_specs=pl.BlockSpec((None, 2 * na, cw), lambda bi, j: (bi, 0, j)),
        out_shape=jax.ShapeDtypeStruct((b, 2 * na, nb * c), BF16),
        compiler_params=_cparams(("parallel", "parallel")),
        name="fnet_a",
    )(x2, ta)
    ys = y.reshape(b, 2, na, nb, c).transpose(0, 2, 1, 3, 4).reshape(b, na, 2 * nb, c)
    o = pl.pallas_call(
        _fnet_b_kernel,
        grid=(b, na // FNET_KA_PER_STEP),
        in_specs=[pl.BlockSpec((None, FNET_KA_PER_STEP, 2 * nb, c), lambda bi, k: (bi, k, 0, 0)),
                  pl.BlockSpec((FNET_KA_PER_STEP, 2 * nb, 2 * nb), lambda bi, k: (k, 0, 0)),
                  pl.BlockSpec((2 * c, c), lambda bi, k: (0, 0)),
                  pl.BlockSpec((c, c), lambda bi, k: (0, 0))],
        out_specs=pl.BlockSpec((None, FNET_KA_PER_STEP, nb, c), lambda bi, k: (bi, k, 0, 0)),
        out_shape=jax.ShapeDtypeStruct((b, na, nb, c), BF16),
        compiler_params=_cparams(("parallel", "parallel")),
        name="fnet_b",
    )(ys, gbig, bcs, w_fnet.astype(BF16))
    return o.transpose(0, 2, 1, 3).reshape(b, l, c)


def _dot_split_left(m_bf16, a):
    hi = a.astype(BF16)
    lo = (a - hi.astype(F32)).astype(BF16)
    return (jnp.dot(m_bf16, hi, preferred_element_type=F32)
            + jnp.dot(m_bf16, lo, preferred_element_type=F32))


def _block_diag(x, mask01):
    xb = x.astype(BF16)
    return jnp.concatenate([xb] * N_HEADS, axis=0) * mask01


def _gdn_scan_kernel(xf_ref, gbf_ref, xb_ref, gbb_ref, ex_ref, bdm_ref, of_ref, ob_ref,
                     s_ref, u_ref, wq_ref, qk_ref, kdt_ref, el_ref):
    c = GDN_CHUNK
    ng = xf_ref.shape[0] // c
    step = pl.program_id(1)
    slot_w = step % 2
    slot_r = 1 - slot_w

    @pl.when(step == 0)
    def _():
        for ref in (s_ref, u_ref, wq_ref, qk_ref, kdt_ref, el_ref):
            ref[...] = jnp.zeros_like(ref)

    row = lax.broadcasted_iota(jnp.int32, (c, GROUP_W), 0)
    lane = lax.broadcasted_iota(jnp.int32, (c, GROUP_W), 1)
    colm = lane & (HEAD_DIM - 1)
    lane_head = lane >> HEAD_SHIFT
    incl = [(row >= colm), (row <= colm)]
    strict = [(row > colm), (row < colm)]
    eye_f = (row == colm).astype(F32)
    bd = lambda a: _block_diag(a, bdm_ref[0])

    streams = [(g, d) for g in range(ng) for d in range(2)]

    def rows_of(g, d):
        lc = g if d == 0 else ng - 1 - g
        return pl.ds(lc * c, c)

    x_refs = (xf_ref, xb_ref)
    gb_refs = (gbf_ref, gbb_ref)
    o_refs = (of_ref, ob_ref)

    def pre_phase():
        gb = [gb_refs[d][rows_of(g, d), :] for g, d in streams]
        gb_hi = [a.astype(BF16) for a in gb]
        gb_hl = [jnp.concatenate([hi, (a - hi.astype(F32)).astype(BF16)], axis=1) for a, hi in zip(gb, gb_hi)]
        ex_d = [jnp.dot(jnp.concatenate(gb_hl[d::2], axis=0), ex_ref[d], preferred_element_type=F32)
                for d in range(2)]
        ex = [ex_d[d][g * c:(g + 1) * c] for g, d in streams]
        gc = [a[:, :GROUP_W] for a in ex]
        beta = [a[:, GROUP_W:] for a in ex]
        yield
        gbt = [a.T for a in gb]
        gcr = [jnp.concatenate([t[N_HEADS * d + h:N_HEADS * d + h + 1, :] for h in range(N_HEADS)], axis=1)
               for (g, d), t in zip(streams, gbt)]
        decay = [jnp.exp(jnp.where(incl[d], a - r, -jnp.inf)) for (g, d), a, r in zip(streams, gc, gcr)]
        last = [c - 1, 0]
        g_last = [a[last[d]:last[d] + 1, :] for (g, d), a in zip(streams, gc)]
        e_g = [jnp.exp(a) for a in gc]
        e_rem = [jnp.exp(gl - a) for gl, a in zip(g_last, gc)]
        for i, gl in enumerate(g_last):
            el_ref[slot_w, i] = jnp.broadcast_to(jnp.exp(gl), (SUBLANES, GROUP_W))
        yield
        q = [x_refs[d][rows_of(g, d), 0:GROUP_W] for g, d in streams]
        k = [x_refs[d][rows_of(g, d), GROUP_W:2 * GROUP_W] for g, d in streams]
        v = [x_refs[d][rows_of(g, d), 2 * GROUP_W:3 * GROUP_W] for g, d in streams]
        kb = [a * b for a, b in zip(k, beta)]
        kq = [_dot_nt(jnp.concatenate([a, b], axis=0), bd(kk)) for a, b, kk in zip(kb, q, k)]
        lm = [jnp.where(strict[d], a[:c] * dec, 0.0) for (g, d), a, dec in zip(streams, kq, decay)]
        yield
        for i, (a, dec) in enumerate(zip(kq, decay)):
            qk_ref[slot_w, i] = (a[c:] * dec).astype(BF16)
        for i, (a, e) in enumerate(zip(k, e_rem)):
            kdt_ref[slot_w, i] = (a * e).T.astype(BF16)
        minv = [eye_f - jnp.where((row >> 1) == (colm >> 1), a, 0.0) for a in lm]
        lm_b = [a.astype(BF16) for a in lm]
        sh = 1
        while (2 << sh) <= c:
            t = [_dot(m, _block_diag(a, bdm_ref[sh])) for m, a in zip(minv, lm_b)]
            yield
            t = [_dot(a, bd(m)) for a, m in zip(t, minv)]
            minv = [m - a for m, a in zip(minv, t)]
            yield
            sh += 1
        for i, (m, a, b) in enumerate(zip(minv, v, beta)):
            u_ref[slot_w, i] = _dot(m, bd(a * b))
        yield
        for i, (m, a, e, qq) in enumerate(zip(minv, kb, e_g, q)):
            wq_ref[slot_w, i] = jnp.concatenate([_dot(m, bd(a * e)), qq * e], axis=0).astype(BF16)

    def state_phase():
        for g in range(ng):
            idx = [2 * g, 2 * g + 1]
            state = [s_ref[d] for d in range(2)]
            ws = [jnp.dot(wq_ref[slot_r, i], bd(s), preferred_element_type=F32) for i, s in zip(idx, state)]
            yield
            v_new = [u_ref[slot_r, i] - a[:c] for i, a in zip(idx, ws)]
            o2 = [jnp.dot(qk_ref[slot_r, i], bd(a), preferred_element_type=F32) for i, a in zip(idx, v_new)]
            upd = [jnp.dot(kdt_ref[slot_r, i], a.astype(BF16), preferred_element_type=F32)
                   for i, a in zip(idx, v_new)]
            for d in range(2):
                o_refs[d][rows_of(g, d), :] = (ws[d][c:] + o2[d]).astype(o_refs[d].dtype)
                fold = jnp.where(lane_head == 0, upd[d][0:c], 0.0)
                for h in range(1, N_HEADS):
                    fold = fold + jnp.where(lane_head == h, upd[d][h * c:(h + 1) * c], 0.0)
                s_ref[d] = state[d] * el_ref[slot_r, idx[d], 0:1, :] + fold
            yield

    phases = [pre_phase(), state_phase()]
    while phases:
        for p in list(phases):
            if next(p, StopIteration) is StopIteration:
                phases.remove(p)


def _gate_expanders():
    e = np.zeros((2, 2 * GATE_PAD, 2 * GROUP_W), np.float32)
    for d in range(2):
        for h in range(N_HEADS):
            for half in range(2):
                r0 = half * GATE_PAD
                e[d, r0 + N_HEADS * d + h, h * HEAD_DIM:(h + 1) * HEAD_DIM] = 1.0
                e[d, r0 + N_GATE_COLS // 2 + N_HEADS * d + h, GROUP_W + h * HEAD_DIM:GROUP_W + (h + 1) * HEAD_DIM] = 1.0
    return jnp.asarray(e, dtype=BF16)


def _block_diag_masks():
    i = np.arange(GROUP_W)
    r, cc = i[:, None], i[None, :]
    same_head = (r >> HEAD_SHIFT) == (cc >> HEAD_SHIFT)
    rl, cl = r & (HEAD_DIM - 1), cc & (HEAD_DIM - 1)
    masks = [same_head]
    sh = 1
    while (2 << sh) <= HEAD_DIM:
        masks.append(same_head & ((rl >> (sh + 1)) == (cl >> (sh + 1))) & ((rl >> sh) != (cl >> sh)))
        sh += 1
    return jnp.asarray(np.stack(masks).astype(np.float32), dtype=BF16)


def _gdn_scan(qkvn, gb, ng):
    b, l, c3 = qkvn.shape
    rows = ng * GDN_CHUNK
    nsteps = l // rows
    grp_in = lambda i: jnp.minimum(i, nsteps - 1)
    grp_out = lambda i: jnp.maximum(i - 1, 0)
    fwd = lambda w, f: pl.BlockSpec((None, rows, w), lambda bi, i: (bi, f(i), 0))
    bwd = lambda w, f: pl.BlockSpec((None, rows, w), lambda bi, i: (bi, nsteps - 1 - f(i), 0))
    ns = 2 * ng
    c = GDN_CHUNK
    masks = _block_diag_masks()
    return pl.pallas_call(
        _gdn_scan_kernel,
        grid=(b, nsteps + 1),
        in_specs=[fwd(c3, grp_in), fwd(GATE_PAD, grp_in), bwd(c3, grp_in), bwd(GATE_PAD, grp_in),
                  pl.BlockSpec((2, 2 * GATE_PAD, 2 * GROUP_W), lambda bi, i: (0, 0, 0)),
                  pl.BlockSpec(masks.shape, lambda bi, i: (0, 0, 0))],
        out_specs=[fwd(GROUP_W, grp_out), bwd(GROUP_W, grp_out)],
        out_shape=[jax.ShapeDtypeStruct((b, l, GROUP_W), BF16)] * 2,
        scratch_shapes=[pltpu.VMEM((2, HEAD_DIM, GROUP_W), F32),
                        pltpu.VMEM((2, ns, c, GROUP_W), F32),
                        pltpu.VMEM((2, ns, 2 * c, GROUP_W), BF16),
                        pltpu.VMEM((2, ns, c, GROUP_W), BF16),
                        pltpu.VMEM((2, ns, GROUP_W, c), BF16),
                        pltpu.VMEM((2, ns, SUBLANES, GROUP_W), F32)],
        compiler_params=_cparams(("parallel", "arbitrary")),
        name="gdn_scan",
    )(qkvn, gb, qkvn, gb, _gate_expanders(), masks)


def _head_stack(q):
    lane_head = lax.broadcasted_iota(jnp.int32, q.shape, 1) >> HEAD_SHIFT
    zero = jnp.zeros((), q.dtype)
    return jnp.concatenate([jnp.where(lane_head == h, q, zero) for h in range(N_HEADS)], axis=0)


def _head_unstack(pv, m):
    lane_head = lax.broadcasted_iota(jnp.int32, (m, GROUP_W), 1) >> HEAD_SHIFT
    out = jnp.where(lane_head == 0, pv[0:m], 0.0)
    for h in range(1, N_HEADS):
        out = out + jnp.where(lane_head == h, pv[h * m:(h + 1) * m], 0.0)
    return out


def _attend(qs, ks, vs, biases):
    s = [_dot_nt(a, b) for a, b in zip(qs, ks)]
    s = [a if b is None else a + b for a, b in zip(s, biases)]
    m = [jnp.max(a, axis=-1, keepdims=True) for a in s]
    p = [jnp.exp(a - b) for a, b in zip(s, m)]
    den = [jnp.sum(a, axis=-1, keepdims=True) for a in p]
    pv = [jnp.dot(a.astype(BF16), b, preferred_element_type=F32) for a, b in zip(p, vs)]
    return [a / b for a, b in zip(pv, den)]


def _natten_kernel(q_ref, kp_ref, kc_ref, kn_ref, vp_ref, vc_ref, vn_ref, bias_ref, o_ref, kwin_ref, vwin_ref):
    gi = pl.program_id(1)
    rows = pl.num_programs(1) * NA_KH
    blk = kc_ref.shape[0]
    for t, (kr, vr) in enumerate(((kp_ref, vp_ref), (kc_ref, vc_ref), (kn_ref, vn_ref))):
        kwin_ref[t * blk:(t + 1) * blk, :] = kr[...]
        vwin_ref[t * blk:(t + 1) * blk, :] = vr[...]

    def body(t, carry):
        js = [t * NA_ROWS_PER_ITER + i for i in range(NA_ROWS_PER_ITER)]
        r = [gi * NA_KH + j for j in js]
        rs = [jnp.clip(a - NA_KH // 2, 0, rows - NA_KH) for a in r]
        start = [pl.multiple_of((a - (gi - 1) * NA_KH) * GRID_W, GRID_W) for a in rs]
        qrows = [pl.ds(pl.multiple_of(j * GRID_W, GRID_W), GRID_W) for j in js]
        qs = [_head_stack(q_ref[a, :] * (HEAD_DIM ** -0.5)) for a in qrows]
        kw = [kwin_ref[pl.ds(a, NA_KH * GRID_W), :] for a in start]
        vw = [vwin_ref[pl.ds(a, NA_KH * GRID_W), :] for a in start]
        bias = [bias_ref[a - b] for a, b in zip(r, rs)]
        for a, pv in zip(qrows, _attend(qs, kw, vw, bias)):
            o_ref[a, :] = _head_unstack(pv, GRID_W).astype(o_ref.dtype)
        return carry

    lax.fori_loop(0, NA_KH // NA_ROWS_PER_ITER, body, 0)


def _natten_bias(rpb):
    w = np.arange(GRID_W)
    cs = np.clip(w - NA_KW // 2, 0, GRID_W - NA_KW)
    wk = np.arange(GRID_W)
    in_win = (wk[None, :] >= cs[:, None]) & (wk[None, :] < cs[:, None] + NA_KW)
    col_off = wk[None, :] - w[:, None] + NA_KW - 1
    delta = np.arange(NA_KH)
    i = np.arange(NA_KH)
    row_off = i[None, :] - delta[:, None] + NA_KH - 1
    sel_r = (row_off[:, :, None] == np.arange(2 * NA_KH - 1)).astype(np.float32)
    sel_c = ((col_off[:, :, None] == np.arange(2 * NA_KW - 1)) & in_win[:, :, None]).astype(np.float32)
    t = jnp.einsum('hab,dia,wvb->hdiwv', rpb.astype(F32), sel_r, sel_c, precision=lax.Precision.HIGHEST)
    t = jnp.where(jnp.asarray(in_win)[None, None, None], t, NEG_BIG)
    t = t.transpose(1, 0, 3, 2, 4)
    return t.reshape(NA_KH, N_HEADS * GRID_W, NA_KH * GRID_W).astype(F32)


def _natten(cqkv, bias):
    b, l, _ = cqkv.shape
    blk = NA_KH * GRID_W
    ng = l // blk
    spec = lambda colblk, f: pl.BlockSpec((None, blk, GROUP_W), lambda bi, g: (bi, f(g), colblk))
    cur = lambda g: g
    prev = lambda g: jnp.maximum(g - 1, 0)
    nxt = lambda g: jnp.minimum(g + 1, ng - 1)
    return pl.pallas_call(
        _natten_kernel,
        grid=(b, ng),
        in_specs=[spec(0, cur), spec(1, prev), spec(1, cur), spec(1, nxt),
                  spec(2, prev), spec(2, cur), spec(2, nxt),
                  pl.BlockSpec(bias.shape, lambda bi, g: (0, 0, 0))],
        out_specs=pl.BlockSpec((None, blk, GROUP_W), lambda bi, g: (bi, g, 0)),
        out_shape=jax.ShapeDtypeStruct((b, l, GROUP_W), BF16),
        scratch_shapes=[pltpu.VMEM((3 * blk, GROUP_W), BF16), pltpu.VMEM((3 * blk, GROUP_W), BF16)],
        compiler_params=_cparams(("parallel", "parallel")),
        name="natten",
    )(cqkv, cqkv, cqkv, cqkv, cqkv, cqkv, cqkv, bias)


def _memattn(q_ref, k_ref, v_ref):
    ts = q_ref.shape[0] // MEM_Q_SPLIT
    parts = [pl.ds(i * ts, ts) for i in range(MEM_Q_SPLIT)]
    qs = [_head_stack(q_ref[a, :] * (HEAD_DIM ** -0.5)) for a in parts]
    k = k_ref[...]
    v = v_ref[...]
    pv = _attend(qs, [k] * MEM_Q_SPLIT, [v] * MEM_Q_SPLIT, [None] * MEM_Q_SPLIT)
    return [_head_unstack(a, ts) for a in pv]


def _outproj_kernel(ya_ref, of_ref, ob_ref, yc_ref, dq_ref, z_ref, x_ref, mem_ref, mg_ref, wkv_ref,
                    gg_ref, pg_ref, hs_ref, w_ref, o_ref, k_ref, v_ref):
    @pl.when(pl.program_id(1) == 0)
    def _():
        m = mem_ref[...]
        ms = jnp.mean(m * m, axis=-1, keepdims=True)
        mn = (m * lax.rsqrt(ms + EPS) * mg_ref[...]).astype(BF16)
        kv = jnp.dot(mn, wkv_ref[...], preferred_element_type=F32)
        k_ref[...] = kv[:, :GROUP_W].astype(BF16)
        v_ref[...] = kv[:, GROUP_W:].astype(BF16)

    yd = jnp.concatenate(_memattn(dq_ref, k_ref, v_ref), axis=0)
    o = of_ref[...].astype(F32) + ob_ref[...].astype(F32)
    ms = _dot(o * o, hs_ref[...]) * (1.0 / HEAD_DIM)
    yb = o * lax.rsqrt(ms + EPS) * gg_ref[...]
    acc = None
    for i, y in enumerate((ya_ref[...], yb, yc_ref[...], yd)):
        gated = (y.astype(F32) * _silu(z_ref[:, i * GROUP_W:(i + 1) * GROUP_W].astype(F32))).astype(BF16)
        part = jnp.dot(gated, w_ref[i * GROUP_W:(i + 1) * GROUP_W, :], preferred_element_type=F32)
        acc = part if acc is None else acc + part
    ms = jnp.mean(acc * acc, axis=-1, keepdims=True)
    o_ref[...] = x_ref[...] + acc * lax.rsqrt(ms + EPS) * pg_ref[...]


def _outproj(ya, o_f, o_b, yc, dq, z, x, mem, mem_g, w_mem_kv, gdn_g, post_g, w_out, tm):
    b, l, d = x.shape
    m = mem.shape[1]
    grp = pl.BlockSpec((None, tm, GROUP_W), lambda bi, i: (bi, i, 0))
    full = pl.BlockSpec((None, tm, d), lambda bi, i: (bi, i, 0))
    const = lambda shape: pl.BlockSpec(shape, lambda bi, i: (0, 0))
    return pl.pallas_call(
        _outproj_kernel,
        grid=(b, l // tm),
        in_specs=[grp, grp, grp, grp, grp, full, full,
                  pl.BlockSpec((None, m, d), lambda bi, i: (bi, 0, 0)), const((1, d)), const((d, 2 * GROUP_W)),
                  const((1, GROUP_W)), const((1, d)), const((GROUP_W, GROUP_W)), const((d, d))],
        out_specs=full,
        out_shape=jax.ShapeDtypeStruct((b, l, d), F32),
        scratch_shapes=[pltpu.VMEM((m, GROUP_W), BF16), pltpu.VMEM((m, GROUP_W), BF16)],
        compiler_params=_cparams(("parallel", "arbitrary")),
        name="outproj",
    )(ya, o_f, o_b, yc, dq, z, x, mem, mem_g.reshape(1, d), w_mem_kv.astype(BF16),
      jnp.tile(gdn_g, N_HEADS).reshape(1, GROUP_W), post_g.reshape(1, d), _head_sum_matrix(), w_out.astype(BF16))


def _tile(l, want):
    t = min(want, l)
    assert l % t == 0
    return t


def _layer(x, mem, tables, pre_g, post_g, w_perm, w_fnet, conv_w, a_log, dt_bias, gdn_g, na_bias, mem_g, w_mem_kv,
           w_out):
    l = x.shape[1]
    u, qkvn, gb, cqkv, dq, z = _inproj(x, pre_g, w_perm, conv_w, a_log, dt_bias, _tile(l, 1024))
    ya = _fnet(u, w_fnet, tables)
    o_f, o_b = _gdn_scan(qkvn, gb, GDN_CHUNKS_PER_STEP)
    yc = _natten(cqkv, na_bias)
    return _outproj(ya, o_f, o_b, yc, dq, z, x, mem, mem_g, w_mem_kv, gdn_g, post_g, w_out, _tile(l, 1024))


def kernel(x_prompt, x_sample, mem_prompt, mem_sample, pre_norm_g, post_norm_g, w_in, w_fnet, gdn_conv_w,
           gdn_a_log, gdn_dt_bias, gdn_norm_g, na_rpb, mem_norm_g, w_mem_kv, w_out):
    l = x_prompt.shape[1]
    assert x_prompt.shape[1:] == x_sample.shape[1:] == (l, D_MODEL)
    assert l % (NA_KH * GRID_W) == 0 and l % (GDN_CHUNKS_PER_STEP * GDN_CHUNK) == 0 and l % FFT_NA == 0
    depth = pre_norm_g.shape[0]
    tables = _fnet_tables(l)
    w_perm = [_permute_w_in(w_in[i]) for i in range(depth)]
    na_bias = [_natten_bias(na_rpb[i]) for i in range(depth)]

    def trunk(x, mem):
        for i in range(depth):
            x = _layer(x, mem, tables, pre_norm_g[i], post_norm_g[i], w_perm[i], w_fnet[i], gdn_conv_w[i],
                       gdn_a_log[i], gdn_dt_bias[i], gdn_norm_g[i], na_bias[i], mem_norm_g[i], w_mem_kv[i], w_out[i])
        return x

    return (trunk(x_prompt, mem_prompt), trunk(x_sample, mem_sample))
```

```python
import math

import numpy as np
import jax
import jax.numpy as jnp
from jax import lax
from jax.experimental import pallas as pl
from jax.experimental.pallas import tpu as pltpu

F32 = jnp.float32
BF16 = jnp.bfloat16

D_MODEL = 1024
GROUP_W = 256
HEAD_DIM = 64
HEAD_SHIFT = 6
N_HEADS = 4
FNET_BLOCK_W = 64
GDN_CHUNK = 64
CONV_K = 5
GRID_W = 64
NA_KH = 8
NA_KW = 16
N_GATE_COLS = 16
EPS = 1e-6
NEG_BIG = -1e30

LANES = 128
SUBLANES = 8
GATE_PAD = LANES
D_PROJ = GROUP_W + 3 * GROUP_W + GATE_PAD + 3 * GROUP_W + GROUP_W + 4 * GROUP_W
FFT_NA = 64
FNET_KA_PER_STEP = 16
GDN_CHUNKS_PER_STEP = 4
NA_ROWS_PER_ITER = 8
MEM_Q_SPLIT = 4
VMEM_LIMIT = 56 * 1024 * 1024


def _cparams(sem):
    return pltpu.CompilerParams(dimension_semantics=sem, vmem_limit_bytes=VMEM_LIMIT)


def _dot(a, b):
    return jnp.dot(a.astype(BF16), b.astype(BF16), preferred_element_type=F32)


def _dot_nt(a, b):
    return lax.dot_general(a.astype(BF16), b.astype(BF16), (((1,), (1,)), ((), ())),
                           preferred_element_type=F32)


def _dot_split_left(m_bf16, a):
    hi = a.astype(BF16)
    lo = (a - hi.astype(F32)).astype(BF16)
    return (jnp.dot(m_bf16, hi, preferred_element_type=F32)
            + jnp.dot(m_bf16, lo, preferred_element_type=F32))


def _sigmoid(x):
    return 1.0 / (1.0 + jnp.exp(-x))


def _silu(x):
    return x * _sigmoid(x)


def _head_sum_matrix():
    idx = np.arange(GROUP_W) // HEAD_DIM
    return jnp.asarray((idx[:, None] == idx[None, :]).astype(np.float32), dtype=BF16)


def _inproj_kernel(x_ref, xp_ref, xn_ref, g_ref, w_ref, cw_ref, gp_ref, hs_ref, tri_ref,
                   u_ref, qkv_ref, gb_ref, cqkv_ref, dq_ref, z_ref, win_ref):
    i = pl.program_id(1)
    n = pl.num_programs(1)
    tm = x_ref.shape[0]
    halo = xp_ref.shape[0]
    o_bqkv, o_gate, o_cqkv = GROUP_W, 4 * GROUP_W, 4 * GROUP_W + GATE_PAD
    o_dq, o_z = o_cqkv + 3 * GROUP_W, o_cqkv + 4 * GROUP_W

    def normed(x):
        ms = jnp.mean(x * x, axis=-1, keepdims=True)
        return (x * lax.rsqrt(ms + EPS) * g_ref[...]).astype(BF16)

    def proj(hh, lo, hi):
        return jnp.dot(hh, w_ref[:, lo:hi], preferred_element_type=F32)

    h = normed(x_ref[...])
    h_halo = normed(jnp.concatenate([xp_ref[...], xn_ref[...]], axis=0))
    bq = proj(jnp.concatenate([h, h_halo], axis=0), o_bqkv, o_gate)
    win_ref[halo:halo + tm, :] = bq[:tm]
    win_ref[0:halo, :] = bq[tm:tm + halo] * (i > 0).astype(F32)
    win_ref[halo + tm:, :] = bq[tm + halo:] * (i < n - 1).astype(F32)
    a = proj(h, o_gate, o_cqkv)
    pad = CONV_K // 2
    hs = hs_ref[...]
    ts = tri_ref.shape[1]
    lane = lax.broadcasted_iota(jnp.int32, (ts, GATE_PAD), 1)

    def gdn_front(j):
        rows = pl.ds(j * ts, ts)
        acc = cw_ref[0:1, :] * win_ref[pl.ds(halo - pad + j * ts, ts), :]
        for t in range(1, CONV_K):
            acc = acc + cw_ref[t:t + 1, :] * win_ref[pl.ds(halo - pad + t + j * ts, ts), :]
        act = _silu(acc)
        q = act[:, 0:GROUP_W]
        k = act[:, GROUP_W:2 * GROUP_W]
        qkv_ref[rows, 0:GROUP_W] = (q * lax.rsqrt(_dot(q * q, hs) + EPS) * (HEAD_DIM ** -0.5)).astype(qkv_ref.dtype)
        qkv_ref[rows, GROUP_W:2 * GROUP_W] = (k * lax.rsqrt(_dot(k * k, hs) + EPS)).astype(qkv_ref.dtype)
        qkv_ref[rows, 2 * GROUP_W:] = act[:, 2 * GROUP_W:].astype(qkv_ref.dtype)
        aj = a[j * ts:(j + 1) * ts]
        xg = aj + gp_ref[1:2, :]
        softplus = jnp.maximum(xg, 0.0) + jnp.log(1.0 + jnp.exp(-jnp.abs(xg)))
        g = -jnp.exp(gp_ref[0:1, :]) * softplus
        gcf = _dot_split_left(tri_ref[0], g)
        gcb = _dot_split_left(tri_ref[1], g)
        gb_ref[rows, :] = jnp.where(lane < N_HEADS, gcf, jnp.where(lane < 2 * N_HEADS, gcb, _sigmoid(aj)))

    def store(ref, lo, hi, col=None):
        val = proj(h, lo, hi).astype(ref.dtype)
        if col is None:
            ref[...] = val
        else:
            ref[:, col:col + hi - lo] = val

    pieces = [lambda: store(cqkv_ref, o_cqkv, o_dq),
              lambda: (store(dq_ref, o_dq, o_z), store(u_ref, 0, o_bqkv)),
              lambda: store(z_ref, o_z, o_z + 2 * GROUP_W, 0),
              lambda: store(z_ref, o_z + 2 * GROUP_W, o_z + 4 * GROUP_W, 2 * GROUP_W)]
    nsub = tm // ts
    for j in range(max(nsub, len(pieces))):
        if j < len(pieces):
            pieces[j]()
        if j < nsub:
            gdn_front(j)


def _permute_w_in(w_in):
    g = GROUP_W
    o_az, o_bqkv, o_bz, o_bg = g, 2 * g, 5 * g, 6 * g
    o_cqkv = o_bg + N_GATE_COLS
    o_cz, o_dq, o_dz = o_cqkv + 3 * g, o_cqkv + 4 * g, o_cqkv + 5 * g
    pad = jnp.zeros((D_MODEL, GATE_PAD - N_GATE_COLS), w_in.dtype)
    cols = [w_in[:, 0:g], w_in[:, o_bqkv:o_bz], w_in[:, o_bg:o_cqkv], pad, w_in[:, o_cqkv:o_cz],
            w_in[:, o_dq:o_dz], w_in[:, o_az:o_bqkv], w_in[:, o_bz:o_bg], w_in[:, o_cz:o_dq],
            w_in[:, o_dz:o_dz + g]]
    return jnp.concatenate(cols, axis=1).astype(BF16)


def _inproj(x, pre_g, w_perm, conv_w, a_log, dt_bias, tm):
    b, l, d = x.shape
    widths = (GROUP_W, 3 * GROUP_W, GATE_PAD, 3 * GROUP_W, GROUP_W, 4 * GROUP_W)
    dtypes = (BF16, BF16, F32, BF16, BF16, BF16)
    c3 = 3 * GROUP_W
    halo = SUBLANES
    per = tm // halo
    nh = l // halo
    ts = min(tm, 4 * GDN_CHUNK)
    cw = jnp.zeros((SUBLANES, c3), F32).at[:CONV_K].set(conv_w)
    gp = jnp.zeros((SUBLANES, GATE_PAD), F32)
    gp = gp.at[0, :N_GATE_COLS // 2].set(a_log.reshape(-1)).at[1, :N_GATE_COLS // 2].set(dt_bias.reshape(-1))
    ti = np.arange(ts)
    same_chunk = (ti[:, None] // GDN_CHUNK) == (ti[None, :] // GDN_CHUNK)
    tri = np.stack([same_chunk & (ti[None, :] <= ti[:, None]), same_chunk & (ti[None, :] >= ti[:, None])])
    tri = jnp.asarray(tri.astype(np.float32), dtype=BF16)
    row = lambda n: pl.BlockSpec((None, tm, n), lambda bi, i: (bi, i, 0))
    const = lambda shape: pl.BlockSpec(shape, lambda bi, i: (0,) * len(shape))
    return pl.pallas_call(
        _inproj_kernel,
        grid=(b, l // tm),
        in_specs=[row(d),
                  pl.BlockSpec((None, halo, d), lambda bi, i: (bi, jnp.maximum(i * per - 1, 0), 0)),
                  pl.BlockSpec((None, halo, d), lambda bi, i: (bi, jnp.minimum((i + 1) * per, nh - 1), 0)),
                  const((1, d)), const((d, D_PROJ)), const((SUBLANES, c3)), const((SUBLANES, GATE_PAD)),
                  const((GROUP_W, GROUP_W)), const((2, ts, ts))],
        out_specs=[row(n) for n in widths],
        out_shape=[jax.ShapeDtypeStruct((b, l, n), dt) for n, dt in zip(widths, dtypes)],
        scratch_shapes=[pltpu.VMEM((tm + 2 * halo, c3), F32)],
        compiler_params=_cparams(("parallel", "parallel")),
        name="inproj",
    )(x, x, x, pre_g.reshape(1, d), w_perm, cw, gp, _head_sum_matrix(), tri)


def _fnet_a_kernel(x_ref, t_ref, y_ref):
    y_ref[...] = jnp.dot(t_ref[...], x_ref[...].astype(BF16), preferred_element_type=F32).astype(BF16)


def _fnet_b_kernel(y_ref, g_ref, bcs_ref, wf_ref, o_ref):
    nk, nb = o_ref.shape[0], o_ref.shape[1]
    bc = bcs_ref[:GROUP_W, :]
    bs = bcs_ref[GROUP_W:, :]
    wf = wf_ref[...]
    z = [jnp.dot(g_ref[i], y_ref[i], preferred_element_type=F32) for i in range(nk)]
    f = [jnp.dot(a[:nb].astype(BF16), bc, preferred_element_type=F32)
         + jnp.dot(a[nb:].astype(BF16), bs, preferred_element_type=F32) for a in z]
    for i, a in enumerate(f):
        o_ref[i] = jnp.dot(a.astype(BF16), wf, preferred_element_type=F32).astype(o_ref.dtype)


def _fnet_tables(l):
    na, nb = FFT_NA, l // FFT_NA
    ia = np.arange(na)
    ang_a = 2.0 * np.pi * ((ia[:, None] * ia[None, :]) % na) / na
    ta = np.concatenate([np.cos(ang_a), -np.sin(ang_a)], axis=0) / math.sqrt(na)
    ka = jnp.arange(na, dtype=jnp.int32)[:, None, None]
    kb = jnp.arange(nb, dtype=jnp.int32)[None, :, None]
    n2 = jnp.arange(nb, dtype=jnp.int32)[None, None, :]
    ang = (2.0 * math.pi / l) * ((n2 * (ka + na * kb)) % l).astype(F32)
    cg = jnp.cos(ang) / math.sqrt(nb)
    sg = jnp.sin(ang) / math.sqrt(nb)
    gbig = jnp.concatenate([jnp.concatenate([cg, sg], axis=2),
                            jnp.concatenate([-sg, cg], axis=2)], axis=1).astype(BF16)
    ic = np.arange(GROUP_W)
    same = (ic[:, None] // FNET_BLOCK_W) == (ic[None, :] // FNET_BLOCK_W)
    ang_c = 2.0 * np.pi * (((ic[:, None] % FNET_BLOCK_W) * (ic[None, :] % FNET_BLOCK_W)) % FNET_BLOCK_W) / FNET_BLOCK_W
    bc = np.where(same, np.cos(ang_c), 0.0) / math.sqrt(FNET_BLOCK_W)
    bs = np.where(same, np.sin(ang_c), 0.0) / math.sqrt(FNET_BLOCK_W)
    bcs = np.concatenate([bc, bs], axis=0)
    return jnp.asarray(ta, dtype=F32).astype(BF16), gbig, jnp.asarray(bcs, dtype=F32).astype(BF16)


def _fnet(u, w_fnet, tables):
    b, l, c = u.shape
    na, nb = FFT_NA, l // FFT_NA
    ta, gbig, bcs = tables
    cw = min(8192, nb * c)
    x2 = u.reshape(b, na, nb * c)
    y = pl.pallas_call(
        _fnet_a_kernel,
        grid=(b, nb * c // cw),
        in_specs=[pl.BlockSpec((None, na, cw), lambda bi, j: (bi, 0, j)),
                  pl.BlockSpec((2 * na, na), lambda bi, j: (0, 0))],
        out_specs=pl.BlockSpec((None, 2 * na, cw), lambda bi, j: (bi, 0, j)),
        out_shape=jax.ShapeDtypeStruct((b, 2 * na, nb * c), BF16),
        compiler_params=_cparams(("parallel", "parallel")),
        name="fnet_a",
    )(x2, ta)
    ys = y.reshape(b, 2, na, nb, c).transpose(0, 2, 1, 3, 4).reshape(b, na, 2 * nb, c)
    o = pl.pallas_call(
        _fnet_b_kernel,
        grid=(b, na // FNET_KA_PER_STEP),
        in_specs=[pl.BlockSpec((None, FNET_KA_PER_STEP, 2 * nb, c), lambda bi, k: (bi, k, 0, 0)),
                  pl.BlockSpec((FNET_KA_PER_STEP, 2 * nb, 2 * nb), lambda bi, k: (k, 0, 0)),
                  pl.BlockSpec((2 * c, c), lambda bi, k: (0, 0)),
                  pl.BlockSpec((c, c), lambda bi, k: (0, 0))],
        out_specs=pl.BlockSpec((None, FNET_KA_PER_STEP, nb, c), lambda bi, k: (bi, k, 0, 0)),
        out_shape=jax.ShapeDtypeStruct((b, na, nb, c), BF16),
        compiler_params=_cparams(("parallel", "parallel")),
        name="fnet_b",
    )(ys, gbig, bcs, w_fnet.astype(BF16))
    return o.transpose(0, 2, 1, 3).reshape(b, l, c)


def _block_diag(x, mask01):
    xb = x.astype(BF16)
    return jnp.concatenate([xb] * N_HEADS, axis=0) * mask01


def _gdn_scan_kernel(xf_ref, gbf_ref, xb_ref, gbb_ref, ex_ref, bdm_ref, of_ref, ob_ref,
                     s_ref, u_ref, wq_ref, qk_ref, kdt_ref, el_ref):
    c = GDN_CHUNK
    ng = xf_ref.shape[0] // c
    step = pl.program_id(1)
    slot_w = step % 2
    slot_r = 1 - slot_w

    @pl.when(step == 0)
    def _():
        for ref in (s_ref, u_ref, wq_ref, qk_ref, kdt_ref, el_ref):
            ref[...] = jnp.zeros_like(ref)

    row = lax.broadcasted_iota(jnp.int32, (c, GROUP_W), 0)
    lane = lax.broadcasted_iota(jnp.int32, (c, GROUP_W), 1)
    colm = lane & (HEAD_DIM - 1)
    lane_head = lane >> HEAD_SHIFT
    incl = [(row >= colm), (row <= colm)]
    strict = [(row > colm), (row < colm)]
    eye_f = (row == colm).astype(F32)
    bd = lambda a: _block_diag(a, bdm_ref[0])

    streams = [(g, d) for g in range(ng) for d in range(2)]

    def rows_of(g, d):
        lc = g if d == 0 else ng - 1 - g
        return pl.ds(lc * c, c)

    x_refs = (xf_ref, xb_ref)
    gb_refs = (gbf_ref, gbb_ref)
    o_refs = (of_ref, ob_ref)

    def pre_phase():
        gb = [gb_refs[d][rows_of(g, d), :] for g, d in streams]
        gb_hi = [a.astype(BF16) for a in gb]
        gb_hl = [jnp.concatenate([hi, (a - hi.astype(F32)).astype(BF16)], axis=1) for a, hi in zip(gb, gb_hi)]
        ex_d = [jnp.dot(jnp.concatenate(gb_hl[d::2], axis=0), ex_ref[d], preferred_element_type=F32)
                for d in range(2)]
        ex = [ex_d[d][g * c:(g + 1) * c] for g, d in streams]
        gc = [a[:, :GROUP_W] for a in ex]
        beta = [a[:, GROUP_W:] for a in ex]
        yield
        gbt = [a.T for a in gb]
        gcr = [jnp.concatenate([t[N_HEADS * d + h:N_HEADS * d + h + 1, :] for h in range(N_HEADS)], axis=1)
               for (g, d), t in zip(streams, gbt)]
        decay = [jnp.exp(jnp.where(incl[d], a - r, -jnp.inf)) for (g, d), a, r in zip(streams, gc, gcr)]
        last = [c - 1, 0]
        g_last = [a[last[d]:last[d] + 1, :] for (g, d), a in zip(streams, gc)]
        e_g = [jnp.exp(a) for a in gc]
        e_rem = [jnp.exp(gl - a) for gl, a in zip(g_last, gc)]
        for i, gl in enumerate(g_last):
            el_ref[slot_w, i] = jnp.broadcast_to(jnp.exp(gl), (SUBLANES, GROUP_W))
        yield
        q = [x_refs[d][rows_of(g, d), 0:GROUP_W] for g, d in streams]
        k = [x_refs[d][rows_of(g, d), GROUP_W:2 * GROUP_W] for g, d in streams]
        v = [x_refs[d][rows_of(g, d), 2 * GROUP_W:3 * GROUP_W] for g, d in streams]
        kb = [a * b for a, b in zip(k, beta)]
        kq = [_dot_nt(jnp.concatenate([a, b], axis=0), bd(kk)) for a, b, kk in zip(kb, q, k)]
        lm = [jnp.where(strict[d], a[:c] * dec, 0.0) for (g, d), a, dec in zip(streams, kq, decay)]
        yield
        for i, (a, dec) in enumerate(zip(kq, decay)):
            qk_ref[slot_w, i] = (a[c:] * dec).astype(BF16)
        for i, (a, e) in enumerate(zip(k, e_rem)):
            kdt_ref[slot_w, i] = (a * e).T.astype(BF16)
        minv = [eye_f - jnp.where((row >> 1) == (colm >> 1), a, 0.0) for a in lm]
        lm_b = [a.astype(BF16) for a in lm]
        sh = 1
        while (2 << sh) <= c:
            t = [_dot(m, _block_diag(a, bdm_ref[sh])) for m, a in zip(minv, lm_b)]
            yield
            t = [_dot(a, bd(m)) for a, m in zip(t, minv)]
            minv = [m - a for m, a in zip(minv, t)]
            yield
            sh += 1
        for i, (m, a, b) in enumerate(zip(minv, v, beta)):
            u_ref[slot_w, i] = _dot(m, bd(a * b))
        yield
        for i, (m, a, e, qq) in enumerate(zip(minv, kb, e_g, q)):
            wq_ref[slot_w, i] = jnp.concatenate([_dot(m, bd(a * e)), qq * e], axis=0).astype(BF16)

    def state_phase():
        for g in range(ng):
            idx = [2 * g, 2 * g + 1]
            state = [s_ref[d] for d in range(2)]
            ws = [jnp.dot(wq_ref[slot_r, i], bd(s), preferred_element_type=F32) for i, s in zip(idx, state)]
            yield
            v_new = [u_ref[slot_r, i] - a[:c] for i, a in zip(idx, ws)]
            o2 = [jnp.dot(qk_ref[slot_r, i], bd(a), preferred_element_type=F32) for i, a in zip(idx, v_new)]
            upd = [jnp.dot(kdt_ref[slot_r, i], a.astype(BF16), preferred_element_type=F32)
                   for i, a in zip(idx, v_new)]
            for d in range(2):
                o_refs[d][rows_of(g, d), :] = (ws[d][c:] + o2[d]).astype(o_refs[d].dtype)
                fold = jnp.where(lane_head == 0, upd[d][0:c], 0.0)
                for h in range(1, N_HEADS):
                    fold = fold + jnp.where(lane_head == h, upd[d][h * c:(h + 1) * c], 0.0)
                s_ref[d] = state[d] * el_ref[slot_r, idx[d], 0:1, :] + fold
            yield

    phases = [pre_phase(), state_phase()]
    while phases:
        for p in list(phases):
            if next(p, StopIteration) is StopIteration:
                phases.remove(p)


def _gate_expanders():
    e = np.zeros((2, 2 * GATE_PAD, 2 * GROUP_W), np.float32)
    for d in range(2):
        for h in range(N_HEADS):
            for half in range(2):
                r0 = half * GATE_PAD
                e[d, r0 + N_HEADS * d + h, h * HEAD_DIM:(h + 1) * HEAD_DIM] = 1.0
                e[d, r0 + N_GATE_COLS // 2 + N_HEADS * d + h, GROUP_W + h * HEAD_DIM:GROUP_W + (h + 1) * HEAD_DIM] = 1.0
    return jnp.asarray(e, dtype=BF16)


def _block_diag_masks():
    i = np.arange(GROUP_W)
    r, cc = i[:, None], i[None, :]
    same_head = (r >> HEAD_SHIFT) == (cc >> HEAD_SHIFT)
    rl, cl = r & (HEAD_DIM - 1), cc & (HEAD_DIM - 1)
    masks = [same_head]
    sh = 1
    while (2 << sh) <= HEAD_DIM:
        masks.append(same_head & ((rl >> (sh + 1)) == (cl >> (sh + 1))) & ((rl >> sh) != (cl >> sh)))
        sh += 1
    return jnp.asarray(np.stack(masks).astype(np.float32), dtype=BF16)


def _gdn_scan(qkvn, gb, ng):
    b, l, c3 = qkvn.shape
    rows = ng * GDN_CHUNK
    nsteps = l // rows
    grp_in = lambda i: jnp.minimum(i, nsteps - 1)
    grp_out = lambda i: jnp.maximum(i - 1, 0)
    fwd = lambda w, f: pl.BlockSpec((None, rows, w), lambda bi, i: (bi, f(i), 0))
    bwd = lambda w, f: pl.BlockSpec((None, rows, w), lambda bi, i: (bi, nsteps - 1 - f(i), 0))
    ns = 2 * ng
    c = GDN_CHUNK
    masks = _block_diag_masks()
    return pl.pallas_call(
        _gdn_scan_kernel,
        grid=(b, nsteps + 1),
        in_specs=[fwd(c3, grp_in), fwd(GATE_PAD, grp_in), bwd(c3, grp_in), bwd(GATE_PAD, grp_in),
                  pl.BlockSpec((2, 2 * GATE_PAD, 2 * GROUP_W), lambda bi, i: (0, 0, 0)),
                  pl.BlockSpec(masks.shape, lambda bi, i: (0, 0, 0))],
        out_specs=[fwd(GROUP_W, grp_out), bwd(GROUP_W, grp_out)],
        out_shape=[jax.ShapeDtypeStruct((b, l, GROUP_W), BF16)] * 2,
        scratch_shapes=[pltpu.VMEM((2, HEAD_DIM, GROUP_W), F32),
                        pltpu.VMEM((2, ns, c, GROUP_W), F32),
                        pltpu.VMEM((2, ns, 2 * c, GROUP_W), BF16),
                        pltpu.VMEM((2, ns, c, GROUP_W), BF16),
                        pltpu.VMEM((2, ns, GROUP_W, c), BF16),
                        pltpu.VMEM((2, ns, SUBLANES, GROUP_W), F32)],
        compiler_params=_cparams(("parallel", "arbitrary")),
        name="gdn_scan",
    )(qkvn, gb, qkvn, gb, _gate_expanders(), masks)


def _head_stack(q):
    lane_head = lax.broadcasted_iota(jnp.int32, q.shape, 1) >> HEAD_SHIFT
    zero = jnp.zeros((), q.dtype)
    return jnp.concatenate([jnp.where(lane_head == h, q, zero) for h in range(N_HEADS)], axis=0)


def _head_unstack(pv, m):
    lane_head = lax.broadcasted_iota(jnp.int32, (m, GROUP_W), 1) >> HEAD_SHIFT
    out = jnp.where(lane_head == 0, pv[0:m], 0.0)
    for h in range(1, N_HEADS):
        out = out + jnp.where(lane_head == h, pv[h * m:(h + 1) * m], 0.0)
    return out


def _attend(qs, ks, vs, biases):
    s = [_dot_nt(a, b) for a, b in zip(qs, ks)]
    s = [a if b is None else a + b for a, b in zip(s, biases)]
    m = [jnp.max(a, axis=-1, keepdims=True) for a in s]
    p = [jnp.exp(a - b) for a, b in zip(s, m)]
    den = [jnp.sum(a, axis=-1, keepdims=True) for a in p]
    pv = [jnp.dot(a.astype(BF16), b, preferred_element_type=F32) for a, b in zip(p, vs)]
    return [a / b for a, b in zip(pv, den)]


def _natten_kernel(q_ref, kp_ref, kc_ref, kn_ref, vp_ref, vc_ref, vn_ref, bias_ref, o_ref, kwin_ref, vwin_ref):
    gi = pl.program_id(1)
    rows = pl.num_programs(1) * NA_KH
    blk = kc_ref.shape[0]
    for t, (kr, vr) in enumerate(((kp_ref, vp_ref), (kc_ref, vc_ref), (kn_ref, vn_ref))):
        kwin_ref[t * blk:(t + 1) * blk, :] = kr[...]
        vwin_ref[t * blk:(t + 1) * blk, :] = vr[...]

    def body(t, carry):
        js = [t * NA_ROWS_PER_ITER + i for i in range(NA_ROWS_PER_ITER)]
        r = [gi * NA_KH + j for j in js]
        rs = [jnp.clip(a - NA_KH // 2, 0, rows - NA_KH) for a in r]
        start = [pl.multiple_of((a - (gi - 1) * NA_KH) * GRID_W, GRID_W) for a in rs]
        qrows = [pl.ds(pl.multiple_of(j * GRID_W, GRID_W), GRID_W) for j in js]
        qs = [_head_stack(q_ref[a, :] * (HEAD_DIM ** -0.5)) for a in qrows]
        kw = [kwin_ref[pl.ds(a, NA_KH * GRID_W), :] for a in start]
        vw = [vwin_ref[pl.ds(a, NA_KH * GRID_W), :] for a in start]
        bias = [bias_ref[a - b] for a, b in zip(r, rs)]
        for a, pv in zip(qrows, _attend(qs, kw, vw, bias)):
            o_ref[a, :] = _head_unstack(pv, GRID_W).astype(o_ref.dtype)
        return carry

    lax.fori_loop(0, NA_KH // NA_ROWS_PER_ITER, body, 0)


def _natten_bias(rpb):
    w = np.arange(GRID_W)
    cs = np.clip(w - NA_KW // 2, 0, GRID_W - NA_KW)
    wk = np.arange(GRID_W)
    in_win = (wk[None, :] >= cs[:, None]) & (wk[None, :] < cs[:, None] + NA_KW)
    col_off = wk[None, :] - w[:, None] + NA_KW - 1
    delta = np.arange(NA_KH)
    i = np.arange(NA_KH)
    row_off = i[None, :] - delta[:, None] + NA_KH - 1
    sel_r = (row_off[:, :, None] == np.arange(2 * NA_KH - 1)).astype(np.float32)
    sel_c = ((col_off[:, :, None] == np.arange(2 * NA_KW - 1)) & in_win[:, :, None]).astype(np.float32)
    t = jnp.einsum('hab,dia,wvb->hdiwv', rpb.astype(F32), sel_r, sel_c, precision=lax.Precision.HIGHEST)
    t = jnp.where(jnp.asarray(in_win)[None, None, None], t, NEG_BIG)
    t = t.transpose(1, 0, 3, 2, 4)
    return t.reshape(NA_KH, N_HEADS * GRID_W, NA_KH * GRID_W).astype(F32)


def _natten(cqkv, bias):
    b, l, _ = cqkv.shape
    blk = NA_KH * GRID_W
    ng = l // blk
    spec = lambda colblk, f: pl.BlockSpec((None, blk, GROUP_W), lambda bi, g: (bi, f(g), colblk))
    cur = lambda g: g
    prev = lambda g: jnp.maximum(g - 1, 0)
    nxt = lambda g: jnp.minimum(g + 1, ng - 1)
    return pl.pallas_call(
        _natten_kernel,
        grid=(b, ng),
        in_specs=[spec(0, cur), spec(1, prev), spec(1, cur), spec(1, nxt),
                  spec(2, prev), spec(2, cur), spec(2, nxt),
                  pl.BlockSpec(bias.shape, lambda bi, g: (0, 0, 0))],
        out_specs=pl.BlockSpec((None, blk, GROUP_W), lambda bi, g: (bi, g, 0)),
        out_shape=jax.ShapeDtypeStruct((b, l, GROUP_W), BF16),
        scratch_shapes=[pltpu.VMEM((3 * blk, GROUP_W), BF16), pltpu.VMEM((3 * blk, GROUP_W), BF16)],
        compiler_params=_cparams(("parallel", "parallel")),
        name="natten",
    )(cqkv, cqkv, cqkv, cqkv, cqkv, cqkv, cqkv, bias)


def _memattn(q_ref, k_ref, v_ref):
    ts = q_ref.shape[0] // MEM_Q_SPLIT
    parts = [pl.ds(i * ts, ts) for i in range(MEM_Q_SPLIT)]
    qs = [_head_stack(q_ref[a, :] * (HEAD_DIM ** -0.5)) for a in parts]
    k = k_ref[...]
    v = v_ref[...]
    pv = _attend(qs, [k] * MEM_Q_SPLIT, [v] * MEM_Q_SPLIT, [None] * MEM_Q_SPLIT)
    return [_head_unstack(a, ts) for a in pv]


def _outproj_kernel(ya_ref, of_ref, ob_ref, yc_ref, dq_ref, z_ref, x_ref, mem_ref, mg_ref, wkv_ref,
                    gg_ref, pg_ref, hs_ref, w_ref, o_ref, k_ref, v_ref):
    @pl.when(pl.program_id(1) == 0)
    def _():
        m = mem_ref[...]
        ms = jnp.mean(m * m, axis=-1, keepdims=True)
        mn = (m * lax.rsqrt(ms + EPS) * mg_ref[...]).astype(BF16)
        kv = jnp.dot(mn, wkv_ref[...], preferred_element_type=F32)
        k_ref[...] = kv[:, :GROUP_W].astype(BF16)
        v_ref[...] = kv[:, GROUP_W:].astype(BF16)

    yd = jnp.concatenate(_memattn(dq_ref, k_ref, v_ref), axis=0)
    o = of_ref[...].astype(F32) + ob_ref[...].astype(F32)
    ms = _dot(o * o, hs_ref[...]) * (1.0 / HEAD_DIM)
    yb = o * lax.rsqrt(ms + EPS) * gg_ref[...]
    acc = None
    for i, y in enumerate((ya_ref[...], yb, yc_ref[...], yd)):
        gated = (y.astype(F32) * _silu(z_ref[:, i * GROUP_W:(i + 1) * GROUP_W].astype(F32))).astype(BF16)
        part = jnp.dot(gated, w_ref[i * GROUP_W:(i + 1) * GROUP_W, :], preferred_element_type=F32)
        acc = part if acc is None else acc + part
    ms = jnp.mean(acc * acc, axis=-1, keepdims=True)
    o_ref[...] = x_ref[...] + acc * lax.rsqrt(ms + EPS) * pg_ref[...]


def _outproj(ya, o_f, o_b, yc, dq, z, x, mem, mem_g, w_mem_kv, gdn_g, post_g, w_out, tm):
    b, l, d = x.shape
    m = mem.shape[1]
    grp = pl.BlockSpec((None, tm, GROUP_W), lambda bi, i: (bi, i, 0))
    full = pl.BlockSpec((None, tm, d), lambda bi, i: (bi, i, 0))
    const = lambda shape: pl.BlockSpec(shape, lambda bi, i: (0, 0))
    return pl.pallas_call(
        _outproj_kernel,
        grid=(b, l // tm),
        in_specs=[grp, grp, grp, grp, grp, full, full,
                  pl.BlockSpec((None, m, d), lambda bi, i: (bi, 0, 0)), const((1, d)), const((d, 2 * GROUP_W)),
                  const((1, GROUP_W)), const((1, d)), const((GROUP_W, GROUP_W)), const((d, d))],
        out_specs=full,
        out_shape=jax.ShapeDtypeStruct((b, l, d), F32),
        scratch_shapes=[pltpu.VMEM((m, GROUP_W), BF16), pltpu.VMEM((m, GROUP_W), BF16)],
        compiler_params=_cparams(("parallel", "arbitrary")),
        name="outproj",
    )(ya, o_f, o_b, yc, dq, z, x, mem, mem_g.reshape(1, d), w_mem_kv.astype(BF16),
      jnp.tile(gdn_g, N_HEADS).reshape(1, GROUP_W), post_g.reshape(1, d), _head_sum_matrix(), w_out.astype(BF16))


def _tile(l, want):
    t = min(want, l)
    assert l % t == 0
    return t


def _layer(x, mem, tables, pre_g, post_g, w_perm, w_fnet, conv_w, a_log, dt_bias, gdn_g, na_bias, mem_g, w_mem_kv,
           w_out):
    l = x.shape[1]
    u, qkvn, gb, cqkv, dq, z = _inproj(x, pre_g, w_perm, conv_w, a_log, dt_bias, _tile(l, 1024))
    ya = _fnet(u, w_fnet, tables)
    o_f, o_b = _gdn_scan(qkvn, gb, GDN_CHUNKS_PER_STEP)
    yc = _natten(cqkv, na_bias)
    return _outproj(ya, o_f, o_b, yc, dq, z, x, mem, mem_g, w_mem_kv, gdn_g, post_g, w_out, _tile(l, 1024))


def kernel(x_prompt, x_sample, mem_prompt, mem_sample, pre_norm_g, post_norm_g, w_in, w_fnet, gdn_conv_w,
           gdn_a_log, gdn_dt_bias, gdn_norm_g, na_rpb, mem_norm_g, w_mem_kv, w_out):
    l = x_prompt.shape[1]
    assert x_prompt.shape[1:] == x_sample.shape[1:] == (l, D_MODEL)
    assert l % (NA_KH * GRID_W) == 0 and l % (GDN_CHUNKS_PER_STEP * GDN_CHUNK) == 0 and l % FFT_NA == 0
    depth = pre_norm_g.shape[0]
    tables = _fnet_tables(l)
    w_perm = [_permute_w_in(w_in[i]) for i in range(depth)]
    na_bias = [_natten_bias(na_rpb[i]) for i in range(depth)]

    def trunk(x, mem):
        for i in range(depth):
            x = _layer(x, mem, tables, pre_norm_g[i], post_norm_g[i], w_perm[i], w_fnet[i], gdn_conv_w[i],
                       gdn_a_log[i], gdn_dt_bias[i], gdn_norm_g[i], na_bias[i], mem_norm_g[i], w_mem_kv[i], w_out[i])
        return x

    return (trunk(x_prompt, mem_prompt), trunk(x_sample, mem_sample))
```
